```python
import jax, jax.numpy as jnp
from jax import lax
import numpy as np

D_MODEL = 2048
BATCH = 4
SEQ = 8192
DEPTH = 4

CTX_LEN = 256
GRID_W = 64
MIX_W = 2 * D_MODEL
SSD_W = MIX_W // 2
SSD_HEADS = 32
SSD_HEAD_DIM = SSD_W // SSD_HEADS
SSD_GROUPS = 8
SSD_STATE = 128
SSD_CHUNK = 128
CONV_W = 5
MLP_W = MIX_W - SSD_W
MLP_GROUPS = 16
MLP_GROUP_DIM = MLP_W // MLP_GROUPS
MLP_CHUNK = 128
GN = SSD_GROUPS * SSD_STATE
XBC_W = SSD_W + 2 * GN
DT_W = 2 * SSD_HEADS
IN_W = XBC_W + DT_W + SSD_W + 3 * MLP_W
EPS = 1e-6

kernel_name = "hybrid_ssd_chunkmlp_prefix_dit"


def _rmsnorm(x, g):
    xf = x.astype(jnp.float32)
    r = lax.rsqrt(jnp.mean(xf * xf, axis=-1, keepdims=True) + EPS)
    return (xf * r).astype(x.dtype) * g


def _dwconv_rows(x, w, b, n_rows, row_len):
    bsz, L, C = x.shape
    pad = CONV_W // 2
    xp = jnp.pad(x.reshape(bsz, n_rows, row_len, C), ((0, 0), (0, 0), (pad, pad), (0, 0)))
    y = b
    for k in range(CONV_W):
        y = y + xp[:, :, k:k + row_len] * w[k]
    return y.reshape(bsz, L, C)


def _ssd(xh, dt, A, Bm, Cm, h0, with_output):
    f32 = jnp.float32
    bsz, L, H, P = xh.shape
    G, N, Q = SSD_GROUPS, SSD_STATE, SSD_CHUNK
    R = H // G
    nc = L // Q
    x = xh.astype(f32).reshape(bsz, nc, Q, G, R, P)
    dtc = dt.reshape(bsz, nc, Q, G, R)
    a = dtc * A.reshape(G, R)
    xdt = x * dtc[..., None]
    Bc = Bm.astype(f32).reshape(bsz, nc, Q, G, N)
    Cc = Cm.astype(f32).reshape(bsz, nc, Q, G, N)
    acs = jnp.cumsum(a, axis=2)
    a_tot = acs[:, :, -1]
    decay_to_end = jnp.exp(a_tot[:, :, None] - acs)
    states = jnp.einsum('bckgn,bckgr,bckgrp->bcgrpn', Bc, decay_to_end, xdt)

    def step(h, inp):
        s, at = inp
        return h * jnp.exp(at)[..., None, None] + s, h

    hT, h_starts = lax.scan(step, h0.reshape(bsz, G, R, P, N),
                            (jnp.moveaxis(states, 1, 0), jnp.moveaxis(a_tot, 1, 0)))
    final = hT.reshape(bsz, H, P, N)
    if not with_output:
        return None, final
    h_starts = jnp.moveaxis(h_starts, 0, 1)
    CB = jnp.einsum('bcqgn,bckgn->bcgqk', Cc, Bc)
    acs_t = jnp.moveaxis(acs, 2, -1)
    diff = acs_t[..., :, None] - acs_t[..., None, :]
    mask = jnp.tril(jnp.ones((Q, Q), dtype=bool))
    Lmat = jnp.exp(jnp.where(mask, diff, -jnp.inf))
    y_diag = jnp.einsum('bcgqk,bcgrqk,bckgrp->bcqgrp', CB, Lmat, xdt)
    y_off = jnp.einsum('bcqgn,bcgrpn,bcqgr->bcqgrp', Cc, h_starts, jnp.exp(acs))
    return (y_diag + y_off).reshape(bsz, L, H, P), final


def _ssd_prep(z, conv_w, conv_b, dt_bias, a_log, n_rows, row_len):
    bsz, L, _ = z.shape
    xbc = jax.nn.silu(_dwconv_rows(z[..., :XBC_W], conv_w, conv_b, n_rows, row_len))
    xh = xbc[..., :SSD_W].reshape(bsz, L, SSD_HEADS, SSD_HEAD_DIM)
    Bm = xbc[..., SSD_W:SSD_W + GN].reshape(bsz, L, SSD_GROUPS, SSD_STATE)
    Cm = xbc[..., SSD_W + GN:XBC_W].reshape(bsz, L, SSD_GROUPS, SSD_STATE)
    dt_raw = z[..., XBC_W:XBC_W + DT_W].astype(jnp.float32).reshape(bsz, L, 2, SSD_HEADS)
    dt = jax.nn.softplus(dt_raw + dt_bias.astype(jnp.float32))
    A = -jnp.exp(a_log.astype(jnp.float32))
    return xh, Bm, Cm, dt, A


def _ssd_bidir(xh, Bm, Cm, dt, A, h0f, h0b, with_output):
    flip = lambda t: jnp.flip(t, axis=1)
    y_f, h_f = _ssd(xh, dt[:, :, 0], A[0], Bm, Cm, h0f, with_output)
    y_b, h_b = _ssd(flip(xh), flip(dt[:, :, 1]), A[1], flip(Bm), flip(Cm), h0b, with_output)
    if not with_output:
        return None, h_f, h_b
    return y_f + flip(y_b), h_f, h_b


def _mix_out(z, xh, y_ssd, d_skip, g_ssd, g_v, w_s, b_s, g_mlp, w_out):
    bsz, L, _ = z.shape
    o = XBC_W + DT_W
    z_ssd = z[..., o:o + SSD_W]
    u = z[..., o + SSD_W:o + SSD_W + MLP_W]
    v = z[..., o + SSD_W + MLP_W:o + SSD_W + 2 * MLP_W]
    z_mlp = z[..., o + SSD_W + 2 * MLP_W:]
    y = (y_ssd + d_skip.astype(jnp.float32)[:, None] * xh.astype(jnp.float32))
    y = y.reshape(bsz, L, SSD_W).astype(z.dtype)
    y_a = _rmsnorm(y * jax.nn.silu(z_ssd), g_ssd)
    vn = _rmsnorm(v, g_v).reshape(bsz, L // MLP_CHUNK, MLP_CHUNK, MLP_GROUPS, MLP_GROUP_DIM)
    sg = jnp.einsum('gqk,bckgd->bcqgd', w_s, vn) + jnp.swapaxes(b_s, 0, 1)[:, :, None]
    y_b = _rmsnorm(u * sg.reshape(bsz, L, MLP_W) * jax.nn.silu(z_mlp), g_mlp)
    return jnp.concatenate([y_a, y_b], axis=-1) @ w_out


def setup_inputs(seed: int = 0) -> dict:
    key = jax.random.key(seed)
    ks = jax.random.split(key, 20)
    D = D_MODEL
    nrm = jax.random.normal
    x = nrm(ks[0], (BATCH, SEQ, D), jnp.float32)
    c = nrm(ks[1], (BATCH, D), jnp.float32)
    ctx = nrm(ks[2], (BATCH, CTX_LEN, D), jnp.float32)
    c_ctx = nrm(ks[3], (D,), jnp.float32)
    w_ada = nrm(ks[4], (DEPTH, D, 3 * D), jnp.float32) * (0.5 * D ** -0.5)
    b_ada = 0.01 * nrm(ks[5], (DEPTH, 3 * D), jnp.float32)
    g_pre = 1.0 + 0.05 * nrm(ks[6], (DEPTH, D), jnp.float32)
    g_post = 1.0 + 0.05 * nrm(ks[7], (DEPTH, D), jnp.float32)
    w_in = nrm(ks[8], (DEPTH, D, IN_W), jnp.float32) * D ** -0.5
    conv_w = nrm(ks[9], (DEPTH, CONV_W, XBC_W), jnp.float32) * CONV_W ** -0.5
    conv_b = 0.01 * nrm(ks[10], (DEPTH, XBC_W), jnp.float32)
    u_dt = jax.random.uniform(ks[11], (DEPTH, 2, SSD_HEADS), jnp.float32)
    dt0 = jnp.exp(u_dt * (np.log(0.1) - np.log(0.001)) + np.log(0.001))
    dt_bias = dt0 + jnp.log(-jnp.expm1(-dt0))
    a_log = jnp.log(jax.random.uniform(ks[12], (DEPTH, 2, SSD_HEADS), jnp.float32, 1.0, 16.0))
    d_skip = 1.0 + 0.1 * nrm(ks[13], (DEPTH, SSD_HEADS), jnp.float32)
    g_ssd = 1.0 + 0.05 * nrm(ks[14], (DEPTH, SSD_W), jnp.float32)
    g_v = 1.0 + 0.05 * nrm(ks[15], (DEPTH, MLP_W), jnp.float32)
    w_s = nrm(ks[16], (DEPTH, MLP_GROUPS, MLP_CHUNK, MLP_CHUNK), jnp.float32) * (0.5 * MLP_CHUNK ** -0.5)
    b_s = 1.0 + 0.05 * nrm(ks[17], (DEPTH, MLP_GROUPS, MLP_CHUNK), jnp.float32)
    g_mlp = 1.0 + 0.05 * nrm(ks[18], (DEPTH, MLP_W), jnp.float32)
    w_out = nrm(ks[19], (DEPTH, MIX_W, D), jnp.float32) * MIX_W ** -0.5
    return {"x": x, "c": c, "ctx": ctx, "c_ctx": c_ctx, "w_ada": w_ada, "b_ada": b_ada,
            "g_pre": g_pre, "g_post": g_post, "w_in": w_in, "conv_w": conv_w, "conv_b": conv_b,
            "dt_bias": dt_bias, "a_log": a_log, "d_skip": d_skip, "g_ssd": g_ssd, "g_v": g_v,
            "w_s": w_s, "b_s": b_s, "g_mlp": g_mlp, "w_out": w_out}


def reference(x, c, ctx, c_ctx, w_ada, b_ada, g_pre, g_post, w_in, conv_w, conv_b,
              dt_bias, a_log, d_skip, g_ssd, g_v, w_s, b_s, g_mlp, w_out):
    bsz, L, _ = x.shape
    ROWS = L // GRID_W
    sc = jax.nn.silu(c)
    scc = jax.nn.silu(c_ctx)
    h0 = jnp.zeros((bsz, SSD_HEADS, SSD_HEAD_DIM, SSD_STATE), jnp.float32)
    for l in range(DEPTH):
        last = l == DEPTH - 1
        shift, scale, gate = jnp.split(sc @ w_ada[l] + b_ada[l], 3, axis=-1)
        shift_c, scale_c, gate_c = jnp.split(scc @ w_ada[l] + b_ada[l], 3, axis=-1)
        hc = _rmsnorm(ctx, g_pre[l]) * (1.0 + scale_c) + shift_c
        zc = hc @ (w_in[l][:, :XBC_W + DT_W] if last else w_in[l])
        xh_c, B_c, C_c, dt_c, A = _ssd_prep(zc, conv_w[l], conv_b[l], dt_bias[l], a_log[l], 1, CTX_LEN)
        y_c, h_f, h_b = _ssd_bidir(xh_c, B_c, C_c, dt_c, A, h0, h0, not last)
        hx = _rmsnorm(x, g_pre[l]) * (1.0 + scale[:, None]) + shift[:, None]
        zx = hx @ w_in[l]
        xh, Bm, Cm, dt, A = _ssd_prep(zx, conv_w[l], conv_b[l], dt_bias[l], a_log[l], ROWS, GRID_W)
        y_x, _, _ = _ssd_bidir(xh, Bm, Cm, dt, A, h_f, h_b, True)
        out = _rmsnorm(_mix_out(zx, xh, y_x, d_skip[l], g_ssd[l], g_v[l], w_s[l], b_s[l],
                                g_mlp[l], w_out[l]), g_post[l])
        x = x + gate[:, None] * out
        if not last:
            out_c = _rmsnorm(_mix_out(zc, xh_c, y_c, d_skip[l], g_ssd[l], g_v[l], w_s[l], b_s[l],
                                      g_mlp[l], w_out[l]), g_post[l])
            ctx = ctx + gate_c * out_c
    return x
```

```python
import functools

import jax
import jax.numpy as jnp
import numpy as np
from jax import lax
from jax.experimental import pallas as pl
from jax.experimental.pallas import tpu as pltpu

F32 = jnp.float32
BF16 = jnp.bfloat16

GRID_W = 64
HEADS = 32
HEAD_DIM = 64
GROUPS = 8
HEADS_PER_GROUP = HEADS // GROUPS
STATE = 128
CHUNK = 128
CONV_W = 5
CONV_PAD = CONV_W // 2
MLP_GROUPS = 16
MLP_GROUP_DIM = 128
MLP_CHUNK = 128
EPS = 1e-6

SSD_W = HEADS * HEAD_DIM
GN = GROUPS * STATE
XBC_W = SSD_W + 2 * GN
DT_W = 2 * HEADS
MLP_W = MLP_GROUPS * MLP_GROUP_DIM
GROUP_W = HEADS_PER_GROUP * HEAD_DIM
MAIN_W = XBC_W + SSD_W + 3 * MLP_W
LANES = 128

VMEM_LIMIT = 56 * 1024 * 1024


def _silu(v):
    return v * (1.0 / (1.0 + jnp.exp(-v)))


def _rms(v):
    return v * lax.rsqrt(jnp.mean(v * v, axis=-1, keepdims=True) + EPS)


def _split3(v):
    v1 = v.astype(BF16)
    r1 = v - v1.astype(F32)
    v2 = r1.astype(BF16)
    r2 = r1 - v2.astype(F32)
    return v1, v2, r2.astype(BF16)


def _mod_kernel(cc_ref, w_ref, b_ref, o_ref):
    sc = _silu(cc_ref[...])
    o_ref[0] = jnp.dot(sc, w_ref[0], preferred_element_type=F32,
                       precision=lax.Precision.HIGHEST) + b_ref[0]


def _modulation(cc, w_ada, b_ada):
    depth, d, d3 = w_ada.shape
    tn = 768
    return pl.pallas_call(
        _mod_kernel,
        grid=(depth, d3 // tn),
        in_specs=[
            pl.BlockSpec((8, d), lambda l, j: (0, 0)),
            pl.BlockSpec((1, d, tn), lambda l, j: (l, 0, j)),
            pl.BlockSpec((1, 1, tn), lambda l, j: (l, 0, j)),
        ],
        out_specs=pl.BlockSpec((1, 8, tn), lambda l, j: (l, 0, j)),
        out_shape=jax.ShapeDtypeStruct((depth, 8, d3), F32),
        compiler_params=pltpu.CompilerParams(
            dimension_semantics=("parallel", "parallel"), vmem_limit_bytes=VMEM_LIMIT),
        name="adaln_mod",
    )(cc, w_ada, b_ada.reshape(depth, 1, d3))


def _inproj_kernel(x_ref, sc_ref, sh_ref, gpre_ref, w_ref, wdt_ref, cw_ref, cb_ref,
                   z_ref, dt_ref, hx_ref, *, row_len, n_conv_tiles):
    j = pl.program_id(1)
    tm, tn = z_ref.shape

    @pl.when(j == 0)
    def _():
        h = _rms(x_ref[...]) * gpre_ref[...] * (1.0 + sc_ref[0]) + sh_ref[0]
        hb = h.astype(BF16)
        hx_ref[...] = hb
        dt_ref[...] = jnp.dot(hb, wdt_ref[...], preferred_element_type=F32)

    acc = jnp.dot(hx_ref[...], w_ref[...], preferred_element_type=F32)

    @pl.when(j < n_conv_tiles)
    def _():
        pos = lax.broadcasted_iota(jnp.int32, (tm, tn), 0) & (row_len - 1)
        y = cb_ref[...] + cw_ref[CONV_PAD:CONV_PAD + 1, :] * acc
        for d in (-2, -1, 1, 2):
            shifted = pltpu.roll(acc, (-d) % tm, axis=0)
            valid = (pos >= -d) if d < 0 else (pos < row_len - d)
            y = y + cw_ref[CONV_PAD + d:CONV_PAD + d + 1, :] * jnp.where(valid, shifted, 0.0)
        z_ref[...] = _silu(y).astype(BF16)

    @pl.when(j >= n_conv_tiles)
    def _():
        z_ref[...] = acc.astype(BF16)


def _inproj(x2, scale, shift, g_pre, w_main, w_dt, cw8, cb, *, seq_len, row_len, tm, tn):
    tokens, d = x2.shape
    tiles_per_seq = seq_len // tm
    n_conv_tiles = XBC_W // tn
    kern = functools.partial(_inproj_kernel, row_len=row_len, n_conv_tiles=n_conv_tiles)
    return pl.pallas_call(
        kern,
        grid=(tokens // tm, MAIN_W // tn),
        in_specs=[
            pl.BlockSpec((tm, d), lambda i, j: (i, 0)),
            pl.BlockSpec((1, 1, d), lambda i, j: (i // tiles_per_seq, 0, 0)),
            pl.BlockSpec((1, 1, d), lambda i, j: (i // tiles_per_seq, 0, 0)),
            pl.BlockSpec((1, d), lambda i, j: (0, 0)),
            pl.BlockSpec((d, tn), lambda i, j: (0, j)),
            pl.BlockSpec((d, LANES), lambda i, j: (0, 0)),
            pl.BlockSpec((8, tn), lambda i, j: (0, jnp.minimum(j, n_conv_tiles - 1))),
            pl.BlockSpec((1, tn), lambda i, j: (0, jnp.minimum(j, n_conv_tiles - 1))),
        ],
        out_specs=[
            pl.BlockSpec((tm, tn), lambda i, j: (i, j)),
            pl.BlockSpec((tm, LANES), lambda i, j: (i, 0)),
        ],
        out_shape=[
            jax.ShapeDtypeStruct((tokens, MAIN_W), BF16),
            jax.ShapeDtypeStruct((tokens, LANES), F32),
        ],
        scratch_shapes=[pltpu.VMEM((tm, d), BF16)],
        compiler_params=pltpu.CompilerParams(
            dimension_semantics=("parallel", "arbitrary"), vmem_limit_bytes=VMEM_LIMIT),
        name="inproj",
    )(x2, scale, shift, g_pre, w_main, w_dt, cw8, cb)


def _chunk_prep(dtr, bias, a_neg):
    q = dtr.shape[0]
    xx = dtr + bias
    dt = jnp.maximum(xx, 0.0) + jnp.log1p(jnp.exp(-jnp.abs(xx)))
    a = dt * a_neg
    acat = jnp.concatenate(_split3(a), axis=0)
    row = lax.broadcasted_iota(jnp.int32, (q, q), 0)
    col = lax.broadcasted_iota(jnp.int32, (q, q), 1)
    t_inc = jnp.where(col <= row, 1.0, 0.0).astype(BF16)
    t_suf = jnp.where(col >= row, 1.0, 0.0).astype(BF16)
    incl = jnp.dot(jnp.concatenate([t_inc] * 3, axis=1), acat, preferred_element_type=F32)
    suf = jnp.dot(jnp.concatenate([t_suf] * 3, axis=1), acat, preferred_element_type=F32)
    lane = lax.broadcasted_iota(jnp.int32, (q, LANES), 1)
    fwd = lane < HEADS
    cum = jnp.where(fwd, incl, suf)
    tot = jnp.where(fwd, incl[q - 1:q, :], suf[0:1, :])
    w = dt * jnp.exp(tot - cum)
    lo = lane < DT_W
    pack = jnp.where(lo, cum, pltpu.roll(dt, DT_W, axis=1))
    vals = jnp.where(lo, jnp.exp(cum), pltpu.roll(w, DT_W, axis=1))
    return pack, vals


def _expand(vals, which, e_ref, rows=None):
    v = vals if which == 0 else pltpu.roll(vals, LANES - HEADS * which, axis=1)
    if rows is not None:
        v = v[rows]
    return jnp.dot(jnp.concatenate(_split3(v), axis=1), e_ref[...], preferred_element_type=F32)


def _state_outer(b_g, xw_g):
    return lax.dot_general(b_g, xw_g, (((0,), (0,)), ((), ())), preferred_element_type=F32)


def _bwd_kernel(x_ref, b_ref, dt_ref, bias_ref, a_ref, e_ref, h0_ref, hs_ref, hfin_ref, st_ref):
    s = pl.program_id(1)

    @pl.when(s == 0)
    def _():
        st_ref[...] = h0_ref[0]

    _, vals = _chunk_prep(dt_ref[...], bias_ref[...], a_ref[...])
    w_b = _expand(vals, 3, e_ref)
    dec = _expand(vals, 1, e_ref, rows=slice(0, 8))[0:1, :]
    xw = (x_ref[...].astype(F32) * w_b).astype(BF16)
    hs_ref[0, 0] = st_ref[...].astype(BF16)
    for g in range(GROUPS):
        rows = slice(g * STATE, (g + 1) * STATE)
        cols = slice(g * GROUP_W, (g + 1) * GROUP_W)
        st_ref[rows, :] = st_ref[rows, :] * dec[:, cols] + _state_outer(b_ref[:, rows], xw[:, cols])

    @pl.when(s == pl.num_programs(1) - 1)
    def _():
        hfin_ref[0] = st_ref[...]


def _bwd_scan(zmain, dt, bias, a_neg, e_mat, h0, *, n_seq, seq_len):
    nc = seq_len // CHUNK
    tok = lambda b, s: b * nc + (nc - 1 - s)
    return pl.pallas_call(
        _bwd_kernel,
        grid=(n_seq, nc),
        in_specs=[
            pl.BlockSpec((CHUNK, SSD_W), lambda b, s: (tok(b, s), 0)),
            pl.BlockSpec((CHUNK, GN), lambda b, s: (tok(b, s), SSD_W // GN)),
            pl.BlockSpec((CHUNK, LANES), lambda b, s: (tok(b, s), 0)),
            pl.BlockSpec((1, LANES), lambda b, s: (0, 0)),
            pl.BlockSpec((1, LANES), lambda b, s: (0, 0)),
            pl.BlockSpec((3 * LANES, SSD_W), lambda b, s: (0, 0)),
            pl.BlockSpec((1, GN, GROUP_W), lambda b, s: (b, 0, 0)),
        ],
        out_specs=[
            pl.BlockSpec((1, 1, GN, GROUP_W), lambda b, s: (b, nc - 1 - s, 0, 0)),
            pl.BlockSpec((1, GN, GROUP_W), lambda b, s: (b, 0, 0)),
        ],
        out_shape=[
            jax.ShapeDtypeStruct((n_seq, nc, GN, GROUP_W), BF16),
            jax.ShapeDtypeStruct((n_seq, GN, GROUP_W), F32),
        ],
        scratch_shapes=[pltpu.VMEM((GN, GROUP_W), F32)],
        compiler_params=pltpu.CompilerParams(
            dimension_semantics=("parallel", "arbitrary"), vmem_limit_bytes=VMEM_LIMIT),
        name="ssd_bwd_scan",
    )(zmain, zmain, dt, bias, a_neg, e_mat, h0)


def _ssd_kernel(x_ref, b_ref, c_ref, zs_ref, dt_ref, bias_ref, a_ref, e_ref, h0_ref, hb_ref,
                dskip_ref, gssd_ref, ya_ref, hfin_ref, st_ref, y_ref):
    s = pl.program_id(1)
    q = CHUNK

    @pl.when(s == 0)
    def _():
        st_ref[...] = h0_ref[0]

    pack, vals = _chunk_prep(dt_ref[...], bias_ref[...], a_ref[...])
    pack_t = pack.T
    e_f = _expand(vals, 0, e_ref)
    e_b = _expand(vals, 1, e_ref)
    w_f = _expand(vals, 2, e_ref)
    x = x_ref[...]
    xf = x.astype(F32)
    xw = (xf * w_f).astype(BF16)
    row = lax.broadcasted_iota(jnp.int32, (q, q), 0)
    col = lax.broadcasted_iota(jnp.int32, (q, q), 1)
    lower = col < row
    upper = col > row
    lower_eq = col <= row

    for g in range(GROUPS):
        rows = slice(g * STATE, (g + 1) * STATE)
        cols = slice(g * GROUP_W, (g + 1) * GROUP_W)
        b_g = b_ref[:, rows]
        c_g = c_ref[:, rows]
        cb = lax.dot_general(c_g, b_g, (((1,), (1,)), ((), ())), preferred_element_type=F32)
        ys = []
        for r in range(HEADS_PER_GROUP):
            h = g * HEADS_PER_GROUP + r
            cum_f = jnp.broadcast_to(pack[:, h:h + 1], (q, q))
            cum_b = jnp.broadcast_to(pack[:, HEADS + h:HEADS + h + 1], (q, q))
            arg = jnp.where(lower_eq, cum_f - pack_t[h:h + 1, :], cum_b - pack_t[HEADS + h:HEADS + h + 1, :])
            dt_f = pack_t[2 * HEADS + h:2 * HEADS + h + 1, :]
            dt_b = pack_t[3 * HEADS + h:3 * HEADS + h + 1, :]
            wgt = jnp.where(lower, dt_f, jnp.where(upper, dt_b, dt_f + dt_b))
            m = (cb * jnp.exp(arg) * wgt).astype(BF16)
            ys.append(jnp.dot(m, x[:, h * HEAD_DIM:(h + 1) * HEAD_DIM], preferred_element_type=F32))
        y_diag = jnp.concatenate(ys, axis=1)
        y_off = (jnp.dot(c_g, st_ref[rows, :].astype(BF16), preferred_element_type=F32) * e_f[:, cols]
                 + jnp.dot(c_g, hb_ref[0, 0, rows, :], preferred_element_type=F32) * e_b[:, cols])
        y_ref[:, cols] = y_diag + y_off + dskip_ref[:, cols] * xf[:, cols]
        st_ref[rows, :] = st_ref[rows, :] * e_f[q - 1:q, cols] + _state_outer(b_g, xw[:, cols])

    gated = y_ref[...] * _silu(zs_ref[...].astype(F32))
    ya_ref[...] = (_rms(gated) * gssd_ref[...]).astype(BF16)

    @pl.when(s == pl.num_programs(1) - 1)
    def _():
        hfin_ref[0] = st_ref[...]


def _ssd(zmain, dt, bias, a_neg, e_mat, h0f, hb_starts, dskip, g_ssd, *, n_seq, seq_len):
    nc = seq_len // CHUNK
    tok = lambda b, s: b * nc + s
    tokens = n_seq * seq_len
    return pl.pallas_call(
        _ssd_kernel,
        grid=(n_seq, nc),
        in_specs=[
            pl.BlockSpec((CHUNK, SSD_W), lambda b, s: (tok(b, s), 0)),
            pl.BlockSpec((CHUNK, GN), lambda b, s: (tok(b, s), SSD_W // GN)),
            pl.BlockSpec((CHUNK, GN), lambda b, s: (tok(b, s), SSD_W // GN + 1)),
            pl.BlockSpec((CHUNK, SSD_W), lambda b, s: (tok(b, s), XBC_W // SSD_W)),
            pl.BlockSpec((CHUNK, LANES), lambda b, s: (tok(b, s), 0)),
            pl.BlockSpec((1, LANES), lambda b, s: (0, 0)),
            pl.BlockSpec((1, LANES), lambda b, s: (0, 0)),
            pl.BlockSpec((3 * LANES, SSD_W), lambda b, s: (0, 0)),
            pl.BlockSpec((1, GN, GROUP_W), lambda b, s: (b, 0, 0)),
            pl.BlockSpec((1, 1, GN, GROUP_W), lambda b, s: (b, s, 0, 0)),
            pl.BlockSpec((1, SSD_W), lambda b, s: (0, 0)),
            pl.BlockSpec((1, SSD_W), lambda b, s: (0, 0)),
        ],
        out_specs=[
            pl.BlockSpec((CHUNK, SSD_W), lambda b, s: (tok(b, s), 0)),
            pl.BlockSpec((1, GN, GROUP_W), lambda b, s: (b, 0, 0)),
        ],
        out_shape=[
            jax.ShapeDtypeStruct((tokens, SSD_W), BF16),
            jax.ShapeDtypeStruct((n_seq, GN, GROUP_W), F32),
        ],
        scratch_shapes=[pltpu.VMEM((GN, GROUP_W), F32), pltpu.VMEM((CHUNK, SSD_W), F32)],
        compiler_params=pltpu.CompilerParams(
            dimension_semantics=("parallel", "arbitrary"), vmem_limit_bytes=VMEM_LIMIT),
        name="ssd_main",
    )(zmain, zmain, zmain, zmain, dt, bias, a_neg, e_mat, h0f, hb_starts, dskip, g_ssd)


def _out_kernel(ya_ref, u_ref, v_ref, zm_ref, x_ref, gate_ref, gv_ref, gmlp_ref, gpost_ref,
                ws_ref, bs_ref, wout_ref, o_ref, sg_ref):
    tm = x_ref.shape[0]
    vn = (_rms(v_ref[...].astype(F32)) * gv_ref[...]).astype(BF16)
    for c in range(tm // MLP_CHUNK):
        rows = slice(c * MLP_CHUNK, (c + 1) * MLP_CHUNK)
        for g in range(MLP_GROUPS):
            cols = slice(g * MLP_GROUP_DIM, (g + 1) * MLP_GROUP_DIM)
            sg_ref[rows, cols] = jnp.dot(ws_ref[g], vn[rows, cols], preferred_element_type=F32) + bs_ref[:, cols]
    yb = u_ref[...].astype(F32) * sg_ref[...] * _silu(zm_ref[...].astype(F32))
    yb = (_rms(yb) * gmlp_ref[...]).astype(BF16)
    mixed = (jnp.dot(ya_ref[...], wout_ref[0:SSD_W, :], preferred_element_type=F32)
             + jnp.dot(yb, wout_ref[SSD_W:, :], preferred_element_type=F32))
    o_ref[...] = x_ref[...] + gate_ref[0] * (_rms(mixed) * gpost_ref[...])


def _out(ya, zmain, x2, gate, g_v, g_mlp, g_post, w_s, bs_mat, w_out, *, seq_len, tm):
    tokens, d = x2.shape
    tiles_per_seq = seq_len // tm
    col0 = (XBC_W + SSD_W) // MLP_W
    const2 = lambda i: (0, 0)
    return pl.pallas_call(
        _out_kernel,
        grid=(tokens // tm,),
        in_specs=[
            pl.BlockSpec((tm, SSD_W), lambda i: (i, 0)),
            pl.BlockSpec((tm, MLP_W), lambda i: (i, col0)),
            pl.BlockSpec((tm, MLP_W), lambda i: (i, col0 + 1)),
            pl.BlockSpec((tm, MLP_W), lambda i: (i, col0 + 2)),
            pl.BlockSpec((tm, d), lambda i: (i, 0)),
            pl.BlockSpec((1, 1, d), lambda i: (i // tiles_per_seq, 0, 0)),
            pl.BlockSpec((1, MLP_W), const2),
            pl.BlockSpec((1, MLP_W), const2),
            pl.BlockSpec((1, d), const2),
            pl.BlockSpec((MLP_GROUPS, MLP_CHUNK, MLP_CHUNK), lambda i: (0, 0, 0)),
            pl.BlockSpec((MLP_CHUNK, MLP_W), const2),
            pl.BlockSpec((SSD_W + MLP_W, d), const2, pipeline_mode=pl.Buffered(1)),
        ],
        out_specs=pl.BlockSpec((tm, d), lambda i: (i, 0)),
        out_shape=jax.ShapeDtypeStruct((tokens, d), F32),
        scratch_shapes=[pltpu.VMEM((tm, MLP_W), F32)],
        compiler_params=pltpu.CompilerParams(
            dimension_semantics=("parallel",), vmem_limit_bytes=VMEM_LIMIT),
        name="mix_out",
    )(ya, zmain, zmain, zmain, x2, gate, g_v, g_mlp, g_post, w_s, bs_mat, w_out)


def _expand_matrix():
    j = np.arange(3 * LANES)[:, None] % LANES
    h = np.arange(SSD_W)[None, :] // HEAD_DIM
    return jnp.asarray(j == h, dtype=BF16)


def _stream_layer(x2, p, mod_rows, h0f, h0b, *, n_seq, seq_len, row_len, tm_in, tn_in, tm_out):
    d = x2.shape[1]
    shift = mod_rows[:, None, 0:d]
    scale = mod_rows[:, None, d:2 * d]
    gate = mod_rows[:, None, 2 * d:3 * d]
    zmain, dt = _inproj(x2, scale, shift, p["g_pre"], p["w_main"], p["w_dt"], p["cw8"], p["cb"],
                        seq_len=seq_len, row_len=row_len, tm=tm_in, tn=tn_in)
    hb_starts, hb_fin = _bwd_scan(zmain, dt, p["bias"], p["a_neg"], p["e_mat"], h0b,
                                  n_seq=n_seq, seq_len=seq_len)
    ya, hf_fin = _ssd(zmain, dt, p["bias"], p["a_neg"], p["e_mat"], h0f, hb_starts,
                      p["dskip"], p["g_ssd"], n_seq=n_seq, seq_len=seq_len)
    x_new = _out(ya, zmain, x2, gate, p["g_v"], p["g_mlp"], p["g_post"], p["w_s"], p["bs_mat"],
                 p["w_out"], seq_len=seq_len, tm=tm_out)
    return x_new, hf_fin, hb_fin


def _layer_params(l, g_pre, g_post, w_in, conv_w, conv_b, dt_bias, a_log, d_skip, g_ssd, g_v,
                  w_s, b_s, g_mlp, w_out, e_mat):
    d = w_in.shape[1]
    o = XBC_W + DT_W
    w = w_in[l]
    pad = jnp.zeros((LANES - DT_W,), F32)
    return {
        "g_pre": g_pre[l][None, :],
        "g_post": g_post[l][None, :],
        "w_main": jnp.concatenate([w[:, :XBC_W], w[:, o:]], axis=1).astype(BF16),
        "w_dt": jnp.pad(w[:, XBC_W:o], ((0, 0), (0, LANES - DT_W))).astype(BF16),
        "cw8": jnp.pad(conv_w[l], ((0, 8 - CONV_W), (0, 0))),
        "cb": conv_b[l][None, :],
        "bias": jnp.concatenate([dt_bias[l].reshape(-1), pad])[None, :],
        "a_neg": jnp.concatenate([-jnp.exp(a_log[l].reshape(-1)), pad])[None, :],
        "dskip": jnp.repeat(d_skip[l], HEAD_DIM)[None, :],
        "g_ssd": g_ssd[l][None, :],
        "g_v": g_v[l][None, :],
        "g_mlp": g_mlp[l][None, :],
        "w_s": w_s[l].astype(BF16),
        "bs_mat": jnp.repeat(b_s[l].T, MLP_GROUP_DIM, axis=1),
        "w_out": w_out[l].astype(BF16),
        "e_mat": e_mat,
    }


def kernel(x, c, ctx, c_ctx, w_ada, b_ada, g_pre, g_post, w_in, conv_w, conv_b, dt_bias, a_log,
           d_skip, g_ssd, g_v, w_s, b_s, g_mlp, w_out):
    bsz, seq, d = x.shape
    ctx_len = ctx.shape[1]
    depth = w_in.shape[0]
    assert seq % 512 == 0 and ctx_len % CHUNK == 0 and ctx_len & (ctx_len - 1) == 0
    assert w_in.shape[2] == MAIN_W + DT_W and bsz + 1 <= 8

    cc = jnp.concatenate([c, c_ctx[None, :], jnp.zeros((8 - bsz - 1, d), F32)], axis=0)
    mod = _modulation(cc, w_ada, b_ada)
    e_mat = _expand_matrix()

    xs = x.reshape(bsz * seq, d)
    cs = ctx.reshape(bsz * ctx_len, d)
    zeros_h = jnp.zeros((bsz, GN, GROUP_W), F32)
    for l in range(depth):
        p = _layer_params(l, g_pre, g_post, w_in, conv_w, conv_b, dt_bias, a_log, d_skip, g_ssd,
                          g_v, w_s, b_s, g_mlp, w_out, e_mat)
        mod_c = jnp.broadcast_to(mod[l, bsz:bsz + 1], (bsz, 3 * d))
        cs, h_f, h_b = _stream_layer(cs, p, mod_c, zeros_h, zeros_h, n_seq=bsz, seq_len=ctx_len,
                                     row_len=ctx_len, tm_in=ctx_len, tn_in=1024, tm_out=ctx_len)
        xs, _, _ = _stream_layer(xs, p, mod[l, :bsz], h_f, h_b, n_seq=bsz, seq_len=seq,
                                 row_len=GRID_W, tm_in=512, tn_in=1024, tm_out=256)
    return xs.reshape(bsz, seq, d)
```

```python
import functools

import jax
import jax.numpy as jnp
import numpy as np
from jax import lax
from jax.experimental import pallas as pl
from jax.experimental.pallas import tpu as pltpu

F32 = jnp.float32
BF16 = jnp.bfloat16

GRID_W = 64
HEADS = 32
HEAD_DIM = 64
GROUPS = 8
HEADS_PER_GROUP = HEADS // GROUPS
STATE = 128
CHUNK = 128
CONV_W = 5
CONV_PAD = CONV_W // 2
MLP_GROUPS = 16
MLP_GROUP_DIM = 128
MLP_CHUNK = 128
EPS = 1e-6

SSD_W = HEADS * HEAD_DIM
GN = GROUPS * STATE
XBC_W = SSD_W + 2 * GN
DT_W = 2 * HEADS
MLP_W = MLP_GROUPS * MLP_GROUP_DIM
GROUP_W = HEADS_PER_GROUP * HEAD_DIM
REST_W = SSD_W + 3 * MLP_W
MAIN_W = XBC_W + REST_W
INPROJ_STEPS = 8
CONV_SLAB = XBC_W // INPROJ_STEPS
PLAIN_SLAB = REST_W // INPROJ_STEPS
LANES = 128
SCAN_CHUNKS = 4

VMEM_LIMIT = 56 * 1024 * 1024


def _silu(v):
    return v * (1.0 / (1.0 + jnp.exp(-v)))


def _rms(v):
    return v * lax.rsqrt(jnp.mean(v * v, axis=-1, keepdims=True) + EPS)


def _split3(v):
    v1 = v.astype(BF16)
    r1 = v - v1.astype(F32)
    v2 = r1.astype(BF16)
    r2 = r1 - v2.astype(F32)
    return v1, v2, r2.astype(BF16)


def _mod_kernel(cc_ref, w_ref, b_ref, o_ref):
    sc = _silu(cc_ref[...])
    o_ref[0] = jnp.dot(sc, w_ref[0], preferred_element_type=F32,
                       precision=lax.Precision.HIGHEST) + b_ref[0]


def _modulation(cc, w_ada, b_ada):
    depth, d, d3 = w_ada.shape
    tn = 768
    return pl.pallas_call(
        _mod_kernel,
        grid=(depth, d3 // tn),
        in_specs=[
            pl.BlockSpec((8, d), lambda l, j: (0, 0)),
            pl.BlockSpec((1, d, tn), lambda l, j: (l, 0, j)),
            pl.BlockSpec((1, 1, tn), lambda l, j: (l, 0, j)),
        ],
        out_specs=pl.BlockSpec((1, 8, tn), lambda l, j: (l, 0, j)),
        out_shape=jax.ShapeDtypeStruct((depth, 8, d3), F32),
        compiler_params=pltpu.CompilerParams(
            dimension_semantics=("parallel", "parallel"), vmem_limit_bytes=VMEM_LIMIT),
        name="adaln_mod",
    )(cc, w_ada, b_ada.reshape(depth, 1, d3))


def _inproj_kernel(x_ref, sc_ref, sh_ref, gpre_ref, w_ref, wdt_ref, cw_ref, cb_ref,
                   zx_ref, zr_ref, dt_ref, hx_ref, *, row_len):
    j = pl.program_id(1)
    tm, tc = zx_ref.shape

    @pl.when(j == 0)
    def _():
        h = _rms(x_ref[...]) * gpre_ref[...] * (1.0 + sc_ref[0]) + sh_ref[0]
        hb = h.astype(BF16)
        hx_ref[...] = hb
        dt_ref[...] = jnp.dot(hb, wdt_ref[...], preferred_element_type=F32)

    hx = hx_ref[...]
    acc = jnp.dot(hx, w_ref[:, 0:tc], preferred_element_type=F32)
    zr_ref[...] = jnp.dot(hx, w_ref[:, tc:], preferred_element_type=F32).astype(BF16)

    pos = lax.broadcasted_iota(jnp.int32, (tm, tc), 0) & (row_len - 1)
    y = cb_ref[...] + cw_ref[CONV_PAD:CONV_PAD + 1, :] * acc
    for d in (-2, -1, 1, 2):
        shifted = pltpu.roll(acc, (-d) % tm, axis=0)
        valid = (pos >= -d) if d < 0 else (pos < row_len - d)
        y = y + cw_ref[CONV_PAD + d:CONV_PAD + d + 1, :] * jnp.where(valid, shifted, 0.0)
    zx_ref[...] = _silu(y).astype(BF16)


def _inproj(x2, scale, shift, g_pre, w_mix, w_dt, cw8, cb, *, seq_len, row_len, tm):
    tokens, d = x2.shape
    tiles_per_seq = seq_len // tm
    kern = functools.partial(_inproj_kernel, row_len=row_len)
    return pl.pallas_call(
        kern,
        grid=(tokens // tm, INPROJ_STEPS),
        in_specs=[
            pl.BlockSpec((tm, d), lambda i, j: (i, 0)),
            pl.BlockSpec((1, 1, d), lambda i, j: (i // tiles_per_seq, 0, 0)),
            pl.BlockSpec((1, 1, d), lambda i, j: (i // tiles_per_seq, 0, 0)),
            pl.BlockSpec((1, d), lambda i, j: (0, 0)),
            pl.BlockSpec((d, CONV_SLAB + PLAIN_SLAB), lambda i, j: (0, j)),
            pl.BlockSpec((d, LANES), lambda i, j: (0, 0)),
            pl.BlockSpec((8, CONV_SLAB), lambda i, j: (0, j)),
            pl.BlockSpec((1, CONV_SLAB), lambda i, j: (0, j)),
        ],
        out_specs=[
            pl.BlockSpec((tm, CONV_SLAB), lambda i, j: (i, j)),
            pl.BlockSpec((tm, PLAIN_SLAB), lambda i, j: (i, j)),
            pl.BlockSpec((tm, LANES), lambda i, j: (i, 0)),
        ],
        out_shape=[
            jax.ShapeDtypeStruct((tokens, XBC_W), BF16),
            jax.ShapeDtypeStruct((tokens, REST_W), BF16),
            jax.ShapeDtypeStruct((tokens, LANES), F32),
        ],
        scratch_shapes=[pltpu.VMEM((tm, d), BF16)],
        compiler_params=pltpu.CompilerParams(
            dimension_semantics=("parallel", "arbitrary"), vmem_limit_bytes=VMEM_LIMIT),
        name="inproj",
    )(x2, scale, shift, g_pre, w_mix, w_dt, cw8, cb)


def _chunk_prep(dtr, bias, a_neg):
    q = dtr.shape[0]
    xx = dtr + bias
    dt = jnp.maximum(xx, 0.0) + jnp.log1p(jnp.exp(-jnp.abs(xx)))
    a = dt * a_neg
    acat = jnp.concatenate(_split3(a), axis=0)
    row = lax.broadcasted_iota(jnp.int32, (q, q), 0)
    col = lax.broadcasted_iota(jnp.int32, (q, q), 1)
    t_inc = jnp.where(col <= row, 1.0, 0.0).astype(BF16)
    t_suf = jnp.where(col >= row, 1.0, 0.0).astype(BF16)
    incl = jnp.dot(jnp.concatenate([t_inc] * 3, axis=1), acat, preferred_element_type=F32)
    suf = jnp.dot(jnp.concatenate([t_suf] * 3, axis=1), acat, preferred_element_type=F32)
    lane = lax.broadcasted_iota(jnp.int32, (q, LANES), 1)
    fwd = lane < HEADS
    cum = jnp.where(fwd, incl, suf)
    tot = jnp.where(fwd, incl[q - 1:q, :], suf[0:1, :])
    w = dt * jnp.exp(tot - cum)
    lo = lane < DT_W
    pack = jnp.where(lo, cum, pltpu.roll(dt, DT_W, axis=1))
    vals = jnp.where(lo, jnp.exp(cum), pltpu.roll(w, DT_W, axis=1))
    return pack, vals


def _expand(vals, wanted, e_ref):
    v1 = vals.astype(BF16).astype(F32)
    r1 = vals - v1
    v2 = r1.astype(BF16).astype(F32)
    terms = (v1, v2, r1 - v2)
    packed = []
    for which, rows in wanted:
        n = rows.stop - rows.start
        lane = lax.broadcasted_iota(jnp.int32, (n, LANES), 1)
        p = jnp.zeros((n, LANES), F32)
        for k in (2, 1, 0):
            t = terms[k][rows]
            shift = (HEADS * (k - which)) % LANES
            t = t if shift == 0 else pltpu.roll(t, shift, axis=1)
            p = jnp.where(lane < HEADS * (k + 1), t, p)
        packed.append(p.astype(BF16))
    stacked = packed[0] if len(packed) == 1 else jnp.concatenate(packed, axis=0)
    return jnp.dot(stacked, e_ref[...], preferred_element_type=F32)


def _state_outer(b_g, xw_g):
    return lax.dot_general(b_g, xw_g, (((0,), (0,)), ((), ())), preferred_element_type=F32)


def _bwd_kernel(x_ref, b_ref, dt_ref, bias_ref, a_ref, e_ref, h0_ref, hs_ref, hfin_ref, st_ref, *, ck):
    s = pl.program_id(1)

    @pl.when(s == 0)
    def _():
        st_ref[...] = h0_ref[0]

    q = CHUNK
    for ci in reversed(range(ck)):
        tok = slice(ci * q, (ci + 1) * q)
        _, vals = _chunk_prep(dt_ref[tok, :], bias_ref[...], a_ref[...])
        ex = _expand(vals, [(3, slice(0, q)), (1, slice(0, 16))], e_ref)
        w_b = ex[0:q]
        dec = ex[q:q + 1]
        xw = (x_ref[tok, :].astype(F32) * w_b).astype(BF16)
        hs_ref[0, ci] = st_ref[...].astype(BF16)
        for g in range(GROUPS):
            rows = slice(g * STATE, (g + 1) * STATE)
            cols = slice(g * GROUP_W, (g + 1) * GROUP_W)
            st_ref[rows, :] = st_ref[rows, :] * dec[:, cols] + _state_outer(b_ref[tok, rows], xw[:, cols])

    @pl.when(s == pl.num_programs(1) - 1)
    def _():
        hfin_ref[0] = st_ref[...]


def _bwd_scan(zx, dt, bias, a_neg, e_mat, h0, *, n_seq, seq_len, ck):
    ns = seq_len // (ck * CHUNK)
    nc = seq_len // CHUNK
    tm = ck * CHUNK
    tok = lambda b, s: b * ns + (ns - 1 - s)
    return pl.pallas_call(
        functools.partial(_bwd_kernel, ck=ck),
        grid=(n_seq, ns),
        in_specs=[
            pl.BlockSpec((tm, SSD_W), lambda b, s: (tok(b, s), 0)),
            pl.BlockSpec((tm, GN), lambda b, s: (tok(b, s), SSD_W // GN)),
            pl.BlockSpec((tm, LANES), lambda b, s: (tok(b, s), 0)),
            pl.BlockSpec((1, LANES), lambda b, s: (0, 0)),
            pl.BlockSpec((1, LANES), lambda b, s: (0, 0)),
            pl.BlockSpec((LANES, SSD_W), lambda b, s: (0, 0)),
            pl.BlockSpec((1, GN, GROUP_W), lambda b, s: (b, 0, 0)),
        ],
        out_specs=[
            pl.BlockSpec((1, ck, GN, GROUP_W), lambda b, s: (b, ns - 1 - s, 0, 0)),
            pl.BlockSpec((1, GN, GROUP_W), lambda b, s: (b, 0, 0)),
        ],
        out_shape=[
            jax.ShapeDtypeStruct((n_seq, nc, GN, GROUP_W), BF16),
            jax.ShapeDtypeStruct((n_seq, GN, GROUP_W), F32),
        ],
        scratch_shapes=[pltpu.VMEM((GN, GROUP_W), F32)],
        compiler_params=pltpu.CompilerParams(
            dimension_semantics=("parallel", "arbitrary"), vmem_limit_bytes=VMEM_LIMIT),
        name="ssd_bwd_scan",
    )(zx, zx, dt, bias, a_neg, e_mat, h0)


def _ssd_kernel(x_ref, b_ref, c_ref, zs_ref, dt_ref, bias_ref, a_ref, e_ref, h0_ref, hb_ref,
                dskip_ref, gssd_ref, ya_ref, hfin_ref, st_ref, y_ref, *, ck):
    s = pl.program_id(1)
    q = CHUNK

    @pl.when(s == 0)
    def _():
        st_ref[...] = h0_ref[0]

    row = lax.broadcasted_iota(jnp.int32, (q, q), 0)
    col = lax.broadcasted_iota(jnp.int32, (q, q), 1)
    lower = col < row
    upper = col > row
    lower_eq = col <= row
    first_head = lax.broadcasted_iota(jnp.int32, (q, 2 * HEAD_DIM), 1) < HEAD_DIM

    for ci in range(ck):
        tok = slice(ci * q, (ci + 1) * q)
        pack, vals = _chunk_prep(dt_ref[tok, :], bias_ref[...], a_ref[...])
        pack_t = pack.T
        ex = _expand(vals, [(0, slice(0, q)), (1, slice(0, q)), (2, slice(0, q))], e_ref)
        e_f = ex[0:q]
        e_b = ex[q:2 * q]
        w_f = ex[2 * q:3 * q]
        x = x_ref[tok, :]
        xf = x.astype(F32)
        xw = (xf * w_f).astype(BF16)

        for g in range(GROUPS):
            rows = slice(g * STATE, (g + 1) * STATE)
            cols = slice(g * GROUP_W, (g + 1) * GROUP_W)
            b_g = b_ref[tok, rows]
            c_g = c_ref[tok, rows]
            cb = lax.dot_general(c_g, b_g, (((1,), (1,)), ((), ())), preferred_element_type=F32)
            ms = []
            for r in range(HEADS_PER_GROUP):
                h = g * HEADS_PER_GROUP + r
                cum_f = jnp.broadcast_to(pack[:, h:h + 1], (q, q))
                cum_b = jnp.broadcast_to(pack[:, HEADS + h:HEADS + h + 1], (q, q))
                arg = jnp.where(lower_eq, cum_f - pack_t[h:h + 1, :],
                                cum_b - pack_t[HEADS + h:HEADS + h + 1, :])
                dt_f = pack_t[2 * HEADS + h:2 * HEADS + h + 1, :]
                dt_b = pack_t[3 * HEADS + h:3 * HEADS + h + 1, :]
                wgt = jnp.where(lower, dt_f, jnp.where(upper, dt_b, dt_f + dt_b))
                ms.append((cb * jnp.exp(arg) * wgt).astype(BF16))
            ys = []
            for pr in range(HEADS_PER_GROUP // 2):
                c0 = g * GROUP_W + pr * 2 * HEAD_DIM
                xp = x[:, c0:c0 + 2 * HEAD_DIM]
                zero = jnp.zeros_like(xp)
                rhs = jnp.concatenate([jnp.where(first_head, xp, zero), jnp.where(first_head, zero, xp)], axis=0)
                lhs = jnp.concatenate([ms[2 * pr], ms[2 * pr + 1]], axis=1)
                ys.append(jnp.dot(lhs, rhs, preferred_element_type=F32))
            y_diag = jnp.concatenate(ys, axis=1)
            y_off = (jnp.dot(c_g, st_ref[rows, :].astype(BF16), preferred_element_type=F32) * e_f[:, cols]
                     + jnp.dot(c_g, hb_ref[0, ci, rows, :], preferred_element_type=F32) * e_b[:, cols])
            y_ref[tok, cols] = y_diag + y_off + dskip_ref[:, cols] * xf[:, cols]
            st_ref[rows, :] = st_ref[rows, :] * e_f[q - 1:q, cols] + _state_outer(b_g, xw[:, cols])

        gated = y_ref[tok, :] * _silu(zs_ref[tok, :].astype(F32))
        ya_ref[tok, :] = (_rms(gated) * gssd_ref[...]).astype(BF16)

    @pl.when(s == pl.num_programs(1) - 1)
    def _():
        hfin_ref[0] = st_ref[...]


def _ssd(zx, zr, dt, bias, a_neg, e_mat, h0f, hb_starts, dskip, g_ssd, *, n_seq, seq_len, ck):
    ns = seq_len // (ck * CHUNK)
    tm = ck * CHUNK
    tok = lambda b, s: b * ns + s
    tokens = n_seq * seq_len
    return pl.pallas_call(
        functools.partial(_ssd_kernel, ck=ck),
        grid=(n_seq, ns),
        in_specs=[
            pl.BlockSpec((tm, SSD_W), lambda b, s: (tok(b, s), 0)),
            pl.BlockSpec((tm, GN), lambda b, s: (tok(b, s), SSD_W // GN)),
            pl.BlockSpec((tm, GN), lambda b, s: (tok(b, s), SSD_W // GN + 1)),
            pl.BlockSpec((tm, SSD_W), lambda b, s: (tok(b, s), 0)),
            pl.BlockSpec((tm, LANES), lambda b, s: (tok(b, s), 0)),
            pl.BlockSpec((1, LANES), lambda b, s: (0, 0)),
            pl.BlockSpec((1, LANES), lambda b, s: (0, 0)),
            pl.BlockSpec((LANES, SSD_W), lambda b, s: (0, 0)),
            pl.BlockSpec((1, GN, GROUP_W), lambda b, s: (b, 0, 0)),
            pl.BlockSpec((1, ck, GN, GROUP_W), lambda b, s: (b, s, 0, 0)),
            pl.BlockSpec((1, SSD_W), lambda b, s: (0, 0)),
            pl.BlockSpec((1, SSD_W), lambda b, s: (0, 0)),
        ],
        out_specs=[
            pl.BlockSpec((tm, SSD_W), lambda b, s: (tok(b, s), 0)),
            pl.BlockSpec((1, GN, GROUP_W), lambda b, s: (b, 0, 0)),
        ],
        out_shape=[
            jax.ShapeDtypeStruct((tokens, SSD_W), BF16),
            jax.ShapeDtypeStruct((n_seq, GN, GROUP_W), F32),
        ],
        scratch_shapes=[pltpu.VMEM((GN, GROUP_W), F32), pltpu.VMEM((tm, SSD_W), F32)],
        compiler_params=pltpu.CompilerParams(
            dimension_semantics=("parallel", "arbitrary"), vmem_limit_bytes=VMEM_LIMIT),
        name="ssd_main",
    )(zx, zx, zx, zr, dt, bias, a_neg, e_mat, h0f, hb_starts, dskip, g_ssd)


def _out_kernel(ya_ref, u_ref, v_ref, zm_ref, x_ref, gate_ref, gv_ref, gmlp_ref, gpost_ref,
                ws_ref, bs_ref, wout_ref, o_ref, sg_ref):
    tm = x_ref.shape[0]
    vn = (_rms(v_ref[...].astype(F32)) * gv_ref[...]).astype(BF16)
    for c in range(tm // MLP_CHUNK):
        rows = slice(c * MLP_CHUNK, (c + 1) * MLP_CHUNK)
        for g in range(MLP_GROUPS):
            cols = slice(g * MLP_GROUP_DIM, (g + 1) * MLP_GROUP_DIM)
            sg_ref[rows, cols] = jnp.dot(ws_ref[g], vn[rows, cols], preferred_element_type=F32) + bs_ref[:, cols]
    yb = u_ref[...].astype(F32) * sg_ref[...] * _silu(zm_ref[...].astype(F32))
    yb = (_rms(yb) * gmlp_ref[...]).astype(BF16)
    mixed = (jnp.dot(ya_ref[...], wout_ref[0:SSD_W, :], preferred_element_type=F32)
             + jnp.dot(yb, wout_ref[SSD_W:, :], preferred_element_type=F32))
    o_ref[...] = x_ref[...] + gate_ref[0] * (_rms(mixed) * gpost_ref[...])


def _out(ya, zr, x2, gate, g_v, g_mlp, g_post, w_s, bs_mat, w_out, *, seq_len, tm):
    tokens, d = x2.shape
    tiles_per_seq = seq_len // tm
    col0 = SSD_W // MLP_W
    const2 = lambda i: (0, 0)
    return pl.pallas_call(
        _out_kernel,
        grid=(tokens // tm,),
        in_specs=[
            pl.BlockSpec((tm, SSD_W), lambda i: (i, 0)),
            pl.BlockSpec((tm, MLP_W), lambda i: (i, col0)),
            pl.BlockSpec((tm, MLP_W), lambda i: (i, col0 + 1)),
            pl.BlockSpec((tm, MLP_W), lambda i: (i, col0 + 2)),
            pl.BlockSpec((tm, d), lambda i: (i, 0)),
            pl.BlockSpec((1, 1, d), lambda i: (i // tiles_per_seq, 0, 0)),
            pl.BlockSpec((1, MLP_W), const2),
            pl.BlockSpec((1, MLP_W), const2),
            pl.BlockSpec((1, d), const2),
            pl.BlockSpec((MLP_GROUPS, MLP_CHUNK, MLP_CHUNK), lambda i: (0, 0, 0)),
            pl.BlockSpec((MLP_CHUNK, MLP_W), const2),
            pl.BlockSpec((SSD_W + MLP_W, d), const2, pipeline_mode=pl.Buffered(1)),
        ],
        out_specs=pl.BlockSpec((tm, d), lambda i: (i, 0)),
        out_shape=jax.ShapeDtypeStruct((tokens, d), F32),
        scratch_shapes=[pltpu.VMEM((tm, MLP_W), F32)],
        compiler_params=pltpu.CompilerParams(
            dimension_semantics=("parallel",), vmem_limit_bytes=VMEM_LIMIT),
        name="mix_out",
    )(ya, zr, zr, zr, x2, gate, g_v, g_mlp, g_post, w_s, bs_mat, w_out)


def _expand_matrix():
    j = np.arange(LANES)[:, None]
    h = np.arange(SSD_W)[None, :] // HEAD_DIM
    return jnp.asarray((j % HEADS == h) & (j < 3 * HEADS), dtype=BF16)


def _stream_layer(x2, p, mod_rows, h0f, h0b, *, n_seq, seq_len, row_len, tm_in, tm_out):
    d = x2.shape[1]
    shift = mod_rows[:, None, 0:d]
    scale = mod_rows[:, None, d:2 * d]
    gate = mod_rows[:, None, 2 * d:3 * d]
    zx, zr, dt = _inproj(x2, scale, shift, p["g_pre"], p["w_mix"], p["w_dt"], p["cw8"], p["cb"],
                         seq_len=seq_len, row_len=row_len, tm=tm_in)
    ck = min(SCAN_CHUNKS, seq_len // CHUNK)
    hb_starts, hb_fin = _bwd_scan(zx, dt, p["bias"], p["a_neg"], p["e_mat"], h0b,
                                  n_seq=n_seq, seq_len=seq_len, ck=ck)
    ya, hf_fin = _ssd(zx, zr, dt, p["bias"], p["a_neg"], p["e_mat"], h0f, hb_starts,
                      p["dskip"], p["g_ssd"], n_seq=n_seq, seq_len=seq_len, ck=ck)
    x_new = _out(ya, zr, x2, gate, p["g_v"], p["g_mlp"], p["g_post"], p["w_s"], p["bs_mat"],
                 p["w_out"], seq_len=seq_len, tm=tm_out)
    return x_new, hf_fin, hb_fin


def _layer_params(l, g_pre, g_post, w_in, conv_w, conv_b, dt_bias, a_log, d_skip, g_ssd, g_v,
                  w_s, b_s, g_mlp, w_out, e_mat):
    d = w_in.shape[1]
    o = XBC_W + DT_W
    w = w_in[l]
    pad = jnp.zeros((LANES - DT_W,), F32)
    return {
        "g_pre": g_pre[l][None, :],
        "g_post": g_post[l][None, :],
        "w_mix": jnp.concatenate(
            [w[:, :XBC_W].reshape(d, INPROJ_STEPS, CONV_SLAB), w[:, o:].reshape(d, INPROJ_STEPS, PLAIN_SLAB)],
            axis=2).reshape(d, MAIN_W).astype(BF16),
        "w_dt": jnp.pad(w[:, XBC_W:o], ((0, 0), (0, LANES - DT_W))).astype(BF16),
        "cw8": jnp.pad(conv_w[l], ((0, 8 - CONV_W), (0, 0))),
        "cb": conv_b[l][None, :],
        "bias": jnp.concatenate([dt_bias[l].reshape(-1), pad])[None, :],
        "a_neg": jnp.concatenate([-jnp.exp(a_log[l].reshape(-1)), pad])[None, :],
        "dskip": jnp.repeat(d_skip[l], HEAD_DIM)[None, :],
        "g_ssd": g_ssd[l][None, :],
        "g_v": g_v[l][None, :],
        "g_mlp": g_mlp[l][None, :],
        "w_s": w_s[l].astype(BF16),
        "bs_mat": jnp.repeat(b_s[l].T, MLP_GROUP_DIM, axis=1),
        "w_out": w_out[l].astype(BF16),
        "e_mat": e_mat,
    }


def kernel(x, c, ctx, c_ctx, w_ada, b_ada, g_pre, g_post, w_in, conv_w, conv_b, dt_bias, a_log,
           d_skip, g_ssd, g_v, w_s, b_s, g_mlp, w_out):
    bsz, seq, d = x.shape
    ctx_len = ctx.shape[1]
    depth = w_in.shape[0]
    assert seq % 512 == 0 and ctx_len % CHUNK == 0 and ctx_len & (ctx_len - 1) == 0
    assert w_in.shape[2] == MAIN_W + DT_W and bsz + 1 <= 8

    cc = jnp.concatenate([c, c_ctx[None, :], jnp.zeros((8 - bsz - 1, d), F32)], axis=0)
    mod = _modulation(cc, w_ada, b_ada)
    e_mat = _expand_matrix()

    xs = x.reshape(bsz * seq, d)
    cs = ctx.reshape(bsz * ctx_len, d)
    zeros_h = jnp.zeros((bsz, GN, GROUP_W), F32)
    for l in range(depth):
        p = _layer_params(l, g_pre, g_post, w_in, conv_w, conv_b, dt_bias, a_log, d_skip, g_ssd,
                          g_v, w_s, b_s, g_mlp, w_out, e_mat)
        mod_c = jnp.broadcast_to(mod[l, bsz:bsz + 1], (bsz, 3 * d))
        cs, h_f, h_b = _stream_layer(cs, p, mod_c, zeros_h, zeros_h, n_seq=bsz, seq_len=ctx_len,
                                     row_len=ctx_len, tm_in=ctx_len, tm_out=ctx_len)
        xs, _, _ = _stream_layer(xs, p, mod[l, :bsz], h_f, h_b, n_seq=bsz, seq_len=seq,
                                 row_len=GRID_W, tm_in=512, tm_out=256)
    return xs.reshape(bsz, seq, d)
```

```python
import functools

import jax
import jax.numpy as jnp
import numpy as np
from jax import lax
from jax.experimental import pallas as pl
from jax.experimental.pallas import tpu as pltpu

F32 = jnp.float32
BF16 = jnp.bfloat16

GRID_W = 64
HEADS = 32
HEAD_DIM = 64
GROUPS = 8
HEADS_PER_GROUP = HEADS // GROUPS
STATE = 128
CHUNK = 128
CONV_W = 5
CONV_PAD = CONV_W // 2
MLP_GROUPS = 16
MLP_GROUP_DIM = 128
MLP_CHUNK = 128
EPS = 1e-6

SSD_W = HEADS * HEAD_DIM
GN = GROUPS * STATE
XBC_W = SSD_W + 2 * GN
DT_W = 2 * HEADS
MLP_W = MLP_GROUPS * MLP_GROUP_DIM
GROUP_W = HEADS_PER_GROUP * HEAD_DIM
REST_W = SSD_W + 3 * MLP_W
MAIN_W = XBC_W + REST_W
INPROJ_STEPS = 8
CONV_SLAB = XBC_W // INPROJ_STEPS
PLAIN_SLAB = REST_W // INPROJ_STEPS
LANES = 128
SCAN_CHUNKS = 4
LATENT_TM_IN = 1024
LATENT_TM_OUT = 256

VMEM_LIMIT = 56 * 1024 * 1024


def _silu(v):
    return v * (1.0 / (1.0 + jnp.exp(-v)))


def _rms(v):
    return v * lax.rsqrt(jnp.mean(v * v, axis=-1, keepdims=True) + EPS)


def _split3(v):
    v1 = v.astype(BF16)
    r1 = v - v1.astype(F32)
    v2 = r1.astype(BF16)
    r2 = r1 - v2.astype(F32)
    return v1, v2, r2.astype(BF16)


def _mod_kernel(cc_ref, w_ref, b_ref, o_ref):
    sc = _silu(cc_ref[...])
    o_ref[0] = jnp.dot(sc, w_ref[0], preferred_element_type=F32) + b_ref[0]


def _modulation(cc, w_ada, b_ada):
    depth, d, d3 = w_ada.shape
    tn = 768
    return pl.pallas_call(
        _mod_kernel,
        grid=(depth, d3 // tn),
        in_specs=[
            pl.BlockSpec((8, d), lambda l, j: (0, 0)),
            pl.BlockSpec((1, d, tn), lambda l, j: (l, 0, j)),
            pl.BlockSpec((1, 1, tn), lambda l, j: (l, 0, j)),
        ],
        out_specs=pl.BlockSpec((1, 8, tn), lambda l, j: (l, 0, j)),
        out_shape=jax.ShapeDtypeStruct((depth, 8, d3), F32),
        compiler_params=pltpu.CompilerParams(
            dimension_semantics=("parallel", "parallel"), vmem_limit_bytes=VMEM_LIMIT),
        name="adaln_mod",
    )(cc, w_ada, b_ada.reshape(depth, 1, d3))


def _inproj_kernel(x_ref, gs_ref, sh_ref, wc_ref, wp_ref, wdt_ref, cw_ref, cb_ref,
                   zx_ref, zr_ref, dt_ref, hx_ref, *, row_len):
    j = pl.program_id(1)
    tm, tc = zx_ref.shape

    @pl.when(j == 0)
    def _():
        hb = (_rms(x_ref[...]) * gs_ref[0] + sh_ref[0]).astype(BF16)
        hx_ref[...] = hb
        dt_ref[...] = jnp.dot(hb, wdt_ref[...], preferred_element_type=F32)

    hx = hx_ref[...]
    acc = jnp.dot(hx, wc_ref[...], preferred_element_type=F32)
    zr_ref[...] = jnp.dot(hx, wp_ref[...], preferred_element_type=F32).astype(BF16)

    pos = lax.broadcasted_iota(jnp.int32, (tm, tc), 0) & (row_len - 1)
    y = cb_ref[...] + cw_ref[CONV_PAD:CONV_PAD + 1, :] * acc
    for d in (-2, -1, 1, 2):
        shifted = pltpu.roll(acc, (-d) % tm, axis=0)
        valid = (pos >= -d) if d < 0 else (pos < row_len - d)
        y = y + cw_ref[CONV_PAD + d:CONV_PAD + d + 1, :] * jnp.where(valid, shifted, 0.0)
    zx_ref[...] = _silu(y).astype(BF16)


def _inproj(x2, gs, shift, wts, layer, *, seq_len, row_len, tm):
    tokens, d = x2.shape
    assert gs.shape[0] == 1 or seq_len % tm == 0
    mod_row = (lambda i: 0) if gs.shape[0] == 1 else (lambda i: (i * tm) // seq_len)
    kern = functools.partial(_inproj_kernel, row_len=row_len)
    return pl.pallas_call(
        kern,
        grid=(tokens // tm, INPROJ_STEPS),
        in_specs=[
            pl.BlockSpec((tm, d), lambda i, j: (i, 0)),
            pl.BlockSpec((1, 1, d), lambda i, j: (mod_row(i), 0, 0)),
            pl.BlockSpec((1, 1, d), lambda i, j: (mod_row(i), 0, 0)),
            pl.BlockSpec((None, d, CONV_SLAB), lambda i, j: (layer, 0, j)),
            pl.BlockSpec((None, d, PLAIN_SLAB), lambda i, j: (layer, 0, j)),
            pl.BlockSpec((None, d, LANES), lambda i, j: (layer, 0, 0)),
            pl.BlockSpec((None, 8, CONV_SLAB), lambda i, j: (layer, 0, j)),
            pl.BlockSpec((None, 1, CONV_SLAB), lambda i, j: (layer, 0, j)),
        ],
        out_specs=[
            pl.BlockSpec((tm, CONV_SLAB), lambda i, j: (i, j)),
            pl.BlockSpec((tm, PLAIN_SLAB), lambda i, j: (i, j)),
            pl.BlockSpec((tm, LANES), lambda i, j: (i, 0)),
        ],
        out_shape=[
            jax.ShapeDtypeStruct((tokens, XBC_W), BF16),
            jax.ShapeDtypeStruct((tokens, REST_W), BF16),
            jax.ShapeDtypeStruct((tokens, LANES), F32),
        ],
        scratch_shapes=[pltpu.VMEM((tm, d), BF16)],
        compiler_params=pltpu.CompilerParams(
            dimension_semantics=("parallel", "arbitrary"), vmem_limit_bytes=VMEM_LIMIT),
        name="inproj",
    )(x2, gs, shift, wts["w_conv"], wts["w_plain"], wts["w_dt"], wts["cw8"], wts["cb"])


def _chunk_prep(dtr, bias, a_neg):
    q = dtr.shape[0]
    xx = dtr + bias
    dt = jnp.maximum(xx, 0.0) + jnp.log1p(jnp.exp(-jnp.abs(xx)))
    a = dt * a_neg
    acat = jnp.concatenate(_split3(a), axis=0)
    row = lax.broadcasted_iota(jnp.int32, (q, q), 0)
    col = lax.broadcasted_iota(jnp.int32, (q, q), 1)
    t_inc = jnp.where(col <= row, 1.0, 0.0).astype(BF16)
    t_suf = jnp.where(col >= row, 1.0, 0.0).astype(BF16)
    incl = jnp.dot(jnp.concatenate([t_inc] * 3, axis=1), acat, preferred_element_type=F32)
    suf = jnp.dot(jnp.concatenate([t_suf] * 3, axis=1), acat, preferred_element_type=F32)
    lane = lax.broadcasted_iota(jnp.int32, (q, LANES), 1)
    fwd = lane < HEADS
    cum = jnp.where(fwd, incl, suf)
    tot = jnp.where(fwd, incl[q - 1:q, :], suf[0:1, :])
    w = dt * jnp.exp(tot - cum)
    lo = lane < DT_W
    pack = jnp.where(lo, cum, pltpu.roll(dt, DT_W, axis=1))
    vals = jnp.where(lo, jnp.exp(cum), pltpu.roll(w, DT_W, axis=1))
    return pack, vals


def _expand(vals, wanted, e_ref):
    v1 = vals.astype(BF16).astype(F32)
    r1 = vals - v1
    v2 = r1.astype(BF16).astype(F32)
    terms = (v1, v2, r1 - v2)
    packed = []
    for which, rows in wanted:
        n = rows.stop - rows.start
        lane = lax.broadcasted_iota(jnp.int32, (n, LANES), 1)
        p = jnp.zeros((n, LANES), F32)
        for k in (2, 1, 0):
            t = terms[k][rows]
            shift = (HEADS * (k - which)) % LANES
            t = t if shift == 0 else pltpu.roll(t, shift, axis=1)
            p = jnp.where(lane < HEADS * (k + 1), t, p)
        packed.append(p.astype(BF16))
    stacked = packed[0] if len(packed) == 1 else jnp.concatenate(packed, axis=0)
    return jnp.dot(stacked, e_ref[...], preferred_element_type=F32)


def _state_outer(b_g, xw_g):
    return lax.dot_general(b_g, xw_g, (((0,), (0,)), ((), ())), preferred_element_type=F32)


def _bwd_kernel(x_ref, b_ref, dt_ref, bias_ref, a_ref, e_ref, h0_ref, hs_ref, hfin_ref, st_ref, *, ck):
    s = pl.program_id(1)

    @pl.when(s == 0)
    def _():
        st_ref[...] = h0_ref[0]

    q = CHUNK
    for ci in reversed(range(ck)):
        tok = slice(ci * q, (ci + 1) * q)
        _, vals = _chunk_prep(dt_ref[tok, :], bias_ref[...], a_ref[...])
        ex = _expand(vals, [(3, slice(0, q)), (1, slice(0, 16))], e_ref)
        w_b = ex[0:q]
        dec = ex[q:q + 1]
        xw = (x_ref[tok, :].astype(F32) * w_b).astype(BF16)
        hs_ref[0, ci] = st_ref[...].astype(BF16)
        for g in range(GROUPS):
            rows = slice(g * STATE, (g + 1) * STATE)
            cols = slice(g * GROUP_W, (g + 1) * GROUP_W)
            st_ref[rows, :] = st_ref[rows, :] * dec[:, cols] + _state_outer(b_ref[tok, rows], xw[:, cols])

    @pl.when(s == pl.num_programs(1) - 1)
    def _():
        hfin_ref[0] = st_ref[...]


def _bwd_scan(zx, dt, bias, a_neg, e_mat, h0, *, n_seq, seq_len, ck):
    ns = seq_len // (ck * CHUNK)
    nc = seq_len // CHUNK
    tm = ck * CHUNK
    tok = lambda b, s: b * ns + (ns - 1 - s)
    return pl.pallas_call(
        functools.partial(_bwd_kernel, ck=ck),
        grid=(n_seq, ns),
        in_specs=[
            pl.BlockSpec((tm, SSD_W), lambda b, s: (tok(b, s), 0)),
            pl.BlockSpec((tm, GN), lambda b, s: (tok(b, s), SSD_W // GN)),
            pl.BlockSpec((tm, LANES), lambda b, s: (tok(b, s), 0)),
            pl.BlockSpec((1, LANES), lambda b, s: (0, 0)),
            pl.BlockSpec((1, LANES), lambda b, s: (0, 0)),
            pl.BlockSpec((LANES, SSD_W), lambda b, s: (0, 0)),
            pl.BlockSpec((1, GN, GROUP_W), lambda b, s: (b, 0, 0)),
        ],
        out_specs=[
            pl.BlockSpec((1, ck, GN, GROUP_W), lambda b, s: (b, ns - 1 - s, 0, 0)),
            pl.BlockSpec((1, GN, GROUP_W), lambda b, s: (b, 0, 0)),
        ],
        out_shape=[
            jax.ShapeDtypeStruct((n_seq, nc, GN, GROUP_W), BF16),
            jax.ShapeDtypeStruct((n_seq, GN, GROUP_W), F32),
        ],
        scratch_shapes=[pltpu.VMEM((GN, GROUP_W), F32)],
        compiler_params=pltpu.CompilerParams(
            dimension_semantics=("parallel", "arbitrary"), vmem_limit_bytes=VMEM_LIMIT),
        name="ssd_bwd_scan",
    )(zx, zx, dt, bias, a_neg, e_mat, h0)


def _ssd_kernel(x_ref, b_ref, c_ref, zs_ref, dt_ref, bias_ref, a_ref, e_ref, h0_ref, hb_ref,
                dskip_ref, ya_ref, hfin_ref, st_ref, y_ref, *, ck):
    s = pl.program_id(1)
    q = CHUNK

    @pl.when(s == 0)
    def _():
        st_ref[...] = h0_ref[0]

    row = lax.broadcasted_iota(jnp.int32, (q, q), 0)
    col = lax.broadcasted_iota(jnp.int32, (q, q), 1)
    lower = col < row
    upper = col > row
    lower_eq = col <= row
    first_head = lax.broadcasted_iota(jnp.int32, (q, 2 * HEAD_DIM), 1) < HEAD_DIM

    for ci in range(ck):
        tok = slice(ci * q, (ci + 1) * q)
        pack, vals = _chunk_prep(dt_ref[tok, :], bias_ref[...], a_ref[...])
        pack_t = pack.T
        ex = _expand(vals, [(0, slice(0, q)), (1, slice(0, q)), (2, slice(0, q))], e_ref)
        e_f = ex[0:q]
        e_b = ex[q:2 * q]
        w_f = ex[2 * q:3 * q]
        x = x_ref[tok, :]
        xf = x.astype(F32)
        xw = (xf * w_f).astype(BF16)

        for g in range(GROUPS):
            rows = slice(g * STATE, (g + 1) * STATE)
            cols = slice(g * GROUP_W, (g + 1) * GROUP_W)
            b_g = b_ref[tok, rows]
            c_g = c_ref[tok, rows]
            cb = lax.dot_general(c_g, b_g, (((1,), (1,)), ((), ())), preferred_element_type=F32)
            ms = []
            for r in range(HEADS_PER_GROUP):
                h = g * HEADS_PER_GROUP + r
                cum_f = jnp.broadcast_to(pack[:, h:h + 1], (q, q))
                cum_b = jnp.broadcast_to(pack[:, HEADS + h:HEADS + h + 1], (q, q))
                arg = jnp.where(lower_eq, cum_f - pack_t[h:h + 1, :],
                                cum_b - pack_t[HEADS + h:HEADS + h + 1, :])
                dt_f = pack_t[2 * HEADS + h:2 * HEADS + h + 1, :]
                dt_b = pack_t[3 * HEADS + h:3 * HEADS + h + 1, :]
                wgt = jnp.where(lower, dt_f, jnp.where(upper, dt_b, dt_f + dt_b))
                ms.append((cb * jnp.exp(arg) * wgt).astype(BF16))
            ys = []
            for pr in range(HEADS_PER_GROUP // 2):
                c0 = g * GROUP_W + pr * 2 * HEAD_DIM
                xp = x[:, c0:c0 + 2 * HEAD_DIM]
                zero = jnp.zeros_like(xp)
                rhs = jnp.concatenate([jnp.where(first_head, xp, zero), jnp.where(first_head, zero, xp)], axis=0)
                lhs = jnp.concatenate([ms[2 * pr], ms[2 * pr + 1]], axis=1)
                ys.append(jnp.dot(lhs, rhs, preferred_element_type=F32))
            y_diag = jnp.concatenate(ys, axis=1)
            y_off = (jnp.dot(c_g, st_ref[rows, :].astype(BF16), preferred_element_type=F32) * e_f[:, cols]
                     + jnp.dot(c_g, hb_ref[0, ci, rows, :], preferred_element_type=F32) * e_b[:, cols])
            y_ref[tok, cols] = y_diag + y_off + dskip_ref[:, cols] * xf[:, cols]
            st_ref[rows, :] = st_ref[rows, :] * e_f[q - 1:q, cols] + _state_outer(b_g, xw[:, cols])

        gated = y_ref[tok, :] * _silu(zs_ref[tok, :].astype(F32))
        ya_ref[tok, :] = _rms(gated).astype(BF16)

    @pl.when(s == pl.num_programs(1) - 1)
    def _():
        hfin_ref[0] = st_ref[...]


def _ssd(zx, zr, dt, bias, a_neg, e_mat, h0f, hb_starts, dskip, *, n_seq, seq_len, ck):
    ns = seq_len // (ck * CHUNK)
    tm = ck * CHUNK
    tok = lambda b, s: b * ns + s
    tokens = n_seq * seq_len
    return pl.pallas_call(
        functools.partial(_ssd_kernel, ck=ck),
        grid=(n_seq, ns),
        in_specs=[
            pl.BlockSpec((tm, SSD_W), lambda b, s: (tok(b, s), 0)),
            pl.BlockSpec((tm, GN), lambda b, s: (tok(b, s), SSD_W // GN)),
            pl.BlockSpec((tm, GN), lambda b, s: (tok(b, s), SSD_W // GN + 1)),
            pl.BlockSpec((tm, SSD_W), lambda b, s: (tok(b, s), 0)),
            pl.BlockSpec((tm, LANES), lambda b, s: (tok(b, s), 0)),
            pl.BlockSpec((1, LANES), lambda b, s: (0, 0)),
            pl.BlockSpec((1, LANES), lambda b, s: (0, 0)),
            pl.BlockSpec((LANES, SSD_W), lambda b, s: (0, 0)),
            pl.BlockSpec((1, GN, GROUP_W), lambda b, s: (b, 0, 0)),
            pl.BlockSpec((1, ck, GN, GROUP_W), lambda b, s: (b, s, 0, 0)),
            pl.BlockSpec((1, SSD_W), lambda b, s: (0, 0)),
        ],
        out_specs=[
            pl.BlockSpec((tm, SSD_W), lambda b, s: (tok(b, s), 0)),
            pl.BlockSpec((1, GN, GROUP_W), lambda b, s: (b, 0, 0)),
        ],
        out_shape=[
            jax.ShapeDtypeStruct((tokens, SSD_W), BF16),
            jax.ShapeDtypeStruct((n_seq, GN, GROUP_W), F32),
        ],
        scratch_shapes=[pltpu.VMEM((GN, GROUP_W), F32), pltpu.VMEM((tm, SSD_W), F32)],
        compiler_params=pltpu.CompilerParams(
            dimension_semantics=("parallel", "arbitrary"), vmem_limit_bytes=VMEM_LIMIT),
        name="ssd_main",
    )(zx, zx, zx, zr, dt, bias, a_neg, e_mat, h0f, hb_starts, dskip)


def _out_kernel(ya_ref, u_ref, v_ref, zm_ref, x_ref, gate_ref, gv_ref, gpost_ref,
                ws_ref, bs_ref, wout_ref, o_ref, sg_ref):
    tm = x_ref.shape[0]
    vn = (_rms(v_ref[...].astype(F32)) * gv_ref[...]).astype(BF16)
    for c in range(tm // MLP_CHUNK):
        rows = slice(c * MLP_CHUNK, (c + 1) * MLP_CHUNK)
        for g in range(MLP_GROUPS):
            cols = slice(g * MLP_GROUP_DIM, (g + 1) * MLP_GROUP_DIM)
            sg_ref[rows, cols] = jnp.dot(ws_ref[g], vn[rows, cols], preferred_element_type=F32) + bs_ref[:, cols]
    yb = _rms(u_ref[...].astype(F32) * sg_ref[...] * _silu(zm_ref[...].astype(F32))).astype(BF16)
    mixed = (jnp.dot(ya_ref[...], wout_ref[0:SSD_W, :], preferred_element_type=F32)
             + jnp.dot(yb, wout_ref[SSD_W:, :], preferred_element_type=F32))
    o_ref[...] = x_ref[...] + gate_ref[0] * (_rms(mixed) * gpost_ref[...])


def _out(ya, zr, x2, gate, g_v, g_post, bs_mat, wts, layer, *, seq_len, tm):
    tokens, d = x2.shape
    assert gate.shape[0] == 1 or seq_len % tm == 0
    mod_row = (lambda i: 0) if gate.shape[0] == 1 else (lambda i: (i * tm) // seq_len)
    col0 = SSD_W // MLP_W
    const2 = lambda i: (0, 0)
    return pl.pallas_call(
        _out_kernel,
        grid=(tokens // tm,),
        in_specs=[
            pl.BlockSpec((tm, SSD_W), lambda i: (i, 0)),
            pl.BlockSpec((tm, MLP_W), lambda i: (i, col0)),
            pl.BlockSpec((tm, MLP_W), lambda i: (i, col0 + 1)),
            pl.BlockSpec((tm, MLP_W), lambda i: (i, col0 + 2)),
            pl.BlockSpec((tm, d), lambda i: (i, 0)),
            pl.BlockSpec((1, 1, d), lambda i: (mod_row(i), 0, 0)),
            pl.BlockSpec((1, MLP_W), const2),
            pl.BlockSpec((1, d), const2),
            pl.BlockSpec((None, MLP_GROUPS, MLP_CHUNK, MLP_CHUNK), lambda i: (layer, 0, 0, 0)),
            pl.BlockSpec((MLP_CHUNK, MLP_W), const2),
            pl.BlockSpec((None, SSD_W + MLP_W, d), lambda i: (layer, 0, 0), pipeline_mode=pl.Buffered(1)),
        ],
        out_specs=pl.BlockSpec((tm, d), lambda i: (i, 0)),
        out_shape=jax.ShapeDtypeStruct((tokens, d), F32),
        scratch_shapes=[pltpu.VMEM((tm, MLP_W), F32)],
        compiler_params=pltpu.CompilerParams(
            dimension_semantics=("parallel",), vmem_limit_bytes=VMEM_LIMIT),
        name="mix_out",
    )(ya, zr, zr, zr, x2, gate, g_v, g_post, wts["w_s"], bs_mat, wts["w_out"])


def _expand_matrix():
    j = np.arange(LANES)[:, None]
    h = np.arange(SSD_W)[None, :] // HEAD_DIM
    return jnp.asarray((j % HEADS == h) & (j < 3 * HEADS), dtype=BF16)


def _stream_layer(x2, wts, p, layer, mod_rows, h0f, h0b, *, n_seq, seq_len, row_len, tm_in, tm_out):
    d = x2.shape[1]
    shift = mod_rows[:, None, 0:d]
    gs = p["g_pre"] * (1.0 + mod_rows[:, None, d:2 * d])
    gate = mod_rows[:, None, 2 * d:3 * d]
    zx, zr, dt = _inproj(x2, gs, shift, wts, layer, seq_len=seq_len, row_len=row_len, tm=tm_in)
    ck = min(SCAN_CHUNKS, seq_len // CHUNK)
    hb_starts, hb_fin = _bwd_scan(zx, dt, p["bias"], p["a_neg"], p["e_mat"], h0b,
                                  n_seq=n_seq, seq_len=seq_len, ck=ck)
    ya, hf_fin = _ssd(zx, zr, dt, p["bias"], p["a_neg"], p["e_mat"], h0f, hb_starts,
                      p["dskip"], n_seq=n_seq, seq_len=seq_len, ck=ck)
    x_new = _out(ya, zr, x2, gate, p["g_v"], p["g_post"], p["bs_mat"], wts, layer,
                 seq_len=seq_len, tm=tm_out)
    return x_new, hf_fin, hb_fin


def _all_layer_weights(w_in, conv_w, conv_b, g_ssd, g_mlp, w_s, w_out):
    o = XBC_W + DT_W
    out_gain = jnp.concatenate([g_ssd, g_mlp], axis=1)[:, :, None]
    return {
        "w_conv": w_in[:, :, :XBC_W].astype(BF16),
        "w_plain": w_in[:, :, o:].astype(BF16),
        "w_dt": jnp.pad(w_in[:, :, XBC_W:o], ((0, 0), (0, 0), (0, LANES - DT_W))).astype(BF16),
        "cw8": jnp.pad(conv_w, ((0, 0), (0, 8 - CONV_W), (0, 0))),
        "cb": conv_b[:, None, :],
        "w_s": w_s.astype(BF16),
        "w_out": (w_out * out_gain).astype(BF16),
    }


def _layer_params(l, g_pre, g_post, dt_bias, a_log, d_skip, g_v, b_s, e_mat):
    pad = jnp.zeros((LANES - DT_W,), F32)
    return {
        "g_pre": g_pre[l][None, None, :],
        "g_post": g_post[l][None, :],
        "bias": jnp.concatenate([dt_bias[l].reshape(-1), pad])[None, :],
        "a_neg": jnp.concatenate([-jnp.exp(a_log[l].reshape(-1)), pad])[None, :],
        "dskip": jnp.repeat(d_skip[l], HEAD_DIM)[None, :],
        "g_v": g_v[l][None, :],
        "bs_mat": jnp.repeat(b_s[l].T, MLP_GROUP_DIM, axis=1),
        "e_mat": e_mat,
    }


def kernel(x, c, ctx, c_ctx, w_ada, b_ada, g_pre, g_post, w_in, conv_w, conv_b, dt_bias, a_log,
           d_skip, g_ssd, g_v, w_s, b_s, g_mlp, w_out):
    bsz, seq, d = x.shape
    ctx_len = ctx.shape[1]
    depth = w_in.shape[0]
    assert seq % LATENT_TM_IN == 0 and ctx_len % CHUNK == 0 and ctx_len & (ctx_len - 1) == 0
    assert w_in.shape[2] == MAIN_W + DT_W and bsz + 1 <= 8

    cc = jnp.concatenate([c, c_ctx[None, :], jnp.zeros((8 - bsz - 1, d), F32)], axis=0)
    mod = _modulation(cc, w_ada, b_ada)
    e_mat = _expand_matrix()
    wts = _all_layer_weights(w_in, conv_w, conv_b, g_ssd, g_mlp, w_s, w_out)

    xs = x.reshape(bsz * seq, d)
    cs = ctx.reshape(bsz * ctx_len, d)
    zeros_h = jnp.zeros((bsz, GN, GROUP_W), F32)
    for l in range(depth):
        p = _layer_params(l, g_pre, g_post, dt_bias, a_log, d_skip, g_v, b_s, e_mat)
        cs, h_f, h_b = _stream_layer(cs, wts, p, l, mod[l, bsz:bsz + 1], zeros_h, zeros_h, n_seq=bsz,
                                     seq_len=ctx_len, row_len=ctx_len, tm_in=bsz * ctx_len, tm_out=ctx_len)
        xs, _, _ = _stream_layer(xs, wts, p, l, mod[l, :bsz], h_f, h_b, n_seq=bsz, seq_len=seq,
                                 row_len=GRID_W, tm_in=LATENT_TM_IN, tm_out=LATENT_TM_OUT)
    return xs.reshape(bsz, seq, d)
```

```python
import functools

import jax
import jax.numpy as jnp
import numpy as np
from jax import lax
from jax.experimental import pallas as pl
from jax.experimental.pallas import tpu as pltpu

F32 = jnp.float32
BF16 = jnp.bfloat16

GRID_W = 64
HEADS = 32
HEAD_DIM = 64
GROUPS = 8
HEADS_PER_GROUP = HEADS // GROUPS
STATE = 128
CHUNK = 128
CONV_W = 5
CONV_PAD = CONV_W // 2
MLP_GROUPS = 16
MLP_GROUP_DIM = 128
MLP_CHUNK = 128
EPS = 1e-6
LOG2E = 1.4426950408889634

SSD_W = HEADS * HEAD_DIM
GN = GROUPS * STATE
XBC_W = SSD_W + 2 * GN
DT_W = 2 * HEADS
MLP_W = MLP_GROUPS * MLP_GROUP_DIM
GROUP_W = HEADS_PER_GROUP * HEAD_DIM
REST_W = SSD_W + 3 * MLP_W
MAIN_W = XBC_W + REST_W
INPROJ_STEPS = 8
CONV_SLAB = XBC_W // INPROJ_STEPS
PLAIN_SLAB = REST_W // INPROJ_STEPS
LANES = 128
SCAN_CHUNKS = 4
LATENT_TM_IN = 1024
LATENT_TM_OUT = 256

VMEM_LIMIT = 56 * 1024 * 1024


def _silu(v):
    return v * (1.0 / (1.0 + jnp.exp(-v)))


def _rms(v):
    return v * lax.rsqrt(jnp.mean(v * v, axis=-1, keepdims=True) + EPS)


def _split3(v):
    v1 = v.astype(BF16)
    r1 = v - v1.astype(F32)
    v2 = r1.astype(BF16)
    r2 = r1 - v2.astype(F32)
    return v1, v2, r2.astype(BF16)


def _mod_kernel(cc_ref, w_ref, b_ref, o_ref):
    sc = _silu(cc_ref[...])
    o_ref[0] = jnp.dot(sc, w_ref[0], preferred_element_type=F32) + b_ref[0]


def _modulation(cc, w_ada, b_ada):
    depth, d, d3 = w_ada.shape
    tn = 768
    return pl.pallas_call(
        _mod_kernel,
        grid=(depth, d3 // tn),
        in_specs=[
            pl.BlockSpec((8, d), lambda l, j: (0, 0)),
            pl.BlockSpec((1, d, tn), lambda l, j: (l, 0, j)),
            pl.BlockSpec((1, 1, tn), lambda l, j: (l, 0, j)),
        ],
        out_specs=pl.BlockSpec((1, 8, tn), lambda l, j: (l, 0, j)),
        out_shape=jax.ShapeDtypeStruct((depth, 8, d3), F32),
        compiler_params=pltpu.CompilerParams(
            dimension_semantics=("parallel", "parallel"), vmem_limit_bytes=VMEM_LIMIT),
        name="adaln_mod",
    )(cc, w_ada, b_ada.reshape(depth, 1, d3))


def _inproj_kernel(x_ref, gs_ref, sh_ref, wc_ref, wp_ref, wdt_ref, cw_ref, cb_ref,
                   zx_ref, zr_ref, dt_ref, hx_ref, *, row_len):
    j = pl.program_id(1)
    tm, tc = zx_ref.shape

    @pl.when(j == 0)
    def _():
        hb = (_rms(x_ref[...]) * gs_ref[0] + sh_ref[0]).astype(BF16)
        hx_ref[...] = hb
        dt_ref[...] = jnp.dot(hb, wdt_ref[...], preferred_element_type=F32)

    hx = hx_ref[...]
    acc = jnp.dot(hx, wc_ref[...], preferred_element_type=F32)
    zr_ref[...] = jnp.dot(hx, wp_ref[...], preferred_element_type=F32).astype(BF16)

    pos = lax.broadcasted_iota(jnp.int32, (tm, tc), 0) & (row_len - 1)
    y = cb_ref[...] + cw_ref[CONV_PAD:CONV_PAD + 1, :] * acc
    for d in (-2, -1, 1, 2):
        shifted = pltpu.roll(acc, (-d) % tm, axis=0)
        valid = (pos >= -d) if d < 0 else (pos < row_len - d)
        y = y + cw_ref[CONV_PAD + d:CONV_PAD + d + 1, :] * jnp.where(valid, shifted, 0.0)
    zx_ref[...] = _silu(y).astype(BF16)


def _inproj(x2, gs, shift, wts, layer, *, seq_len, row_len, tm):
    tokens, d = x2.shape
    assert gs.shape[0] == 1 or seq_len % tm == 0
    mod_row = (lambda i: 0) if gs.shape[0] == 1 else (lambda i: (i * tm) // seq_len)
    kern = functools.partial(_inproj_kernel, row_len=row_len)
    return pl.pallas_call(
        kern,
        grid=(tokens // tm, INPROJ_STEPS),
        in_specs=[
            pl.BlockSpec((tm, d), lambda i, j: (i, 0)),
            pl.BlockSpec((1, 1, d), lambda i, j: (mod_row(i), 0, 0)),
            pl.BlockSpec((1, 1, d), lambda i, j: (mod_row(i), 0, 0)),
            pl.BlockSpec((None, d, CONV_SLAB), lambda i, j: (layer, 0, j)),
            pl.BlockSpec((None, d, PLAIN_SLAB), lambda i, j: (layer, 0, XBC_W // PLAIN_SLAB + j)),
            pl.BlockSpec((None, d, LANES), lambda i, j: (layer, 0, MAIN_W // LANES)),
            pl.BlockSpec((None, 8, CONV_SLAB), lambda i, j: (layer, 0, j)),
            pl.BlockSpec((None, 1, CONV_SLAB), lambda i, j: (layer, 0, j)),
        ],
        out_specs=[
            pl.BlockSpec((tm, CONV_SLAB), lambda i, j: (i, j)),
            pl.BlockSpec((tm, PLAIN_SLAB), lambda i, j: (i, j)),
            pl.BlockSpec((tm, LANES), lambda i, j: (i, 0)),
        ],
        out_shape=[
            jax.ShapeDtypeStruct((tokens, XBC_W), BF16),
            jax.ShapeDtypeStruct((tokens, REST_W), BF16),
            jax.ShapeDtypeStruct((tokens, LANES), F32),
        ],
        scratch_shapes=[pltpu.VMEM((tm, d), BF16)],
        compiler_params=pltpu.CompilerParams(
            dimension_semantics=("parallel", "arbitrary"), vmem_limit_bytes=VMEM_LIMIT),
        name="inproj",
    )(x2, gs, shift, wts["w_in"], wts["w_in"], wts["w_in"], wts["cw8"], wts["cb"])


def _chunk_prep(dtr, bias, a_neg):
    q = dtr.shape[0]
    xx = dtr + bias
    dt = jnp.maximum(xx, 0.0) + jnp.log1p(jnp.exp(-jnp.abs(xx)))
    a = dt * a_neg * LOG2E
    acat = jnp.concatenate(_split3(a), axis=0)
    row = lax.broadcasted_iota(jnp.int32, (q, q), 0)
    col = lax.broadcasted_iota(jnp.int32, (q, q), 1)
    t_inc = jnp.where(col <= row, 1.0, 0.0).astype(BF16)
    t_suf = jnp.where(col >= row, 1.0, 0.0).astype(BF16)
    incl = jnp.dot(jnp.concatenate([t_inc] * 3, axis=1), acat, preferred_element_type=F32)
    suf = jnp.dot(jnp.concatenate([t_suf] * 3, axis=1), acat, preferred_element_type=F32)
    lane = lax.broadcasted_iota(jnp.int32, (q, LANES), 1)
    fwd = lane < HEADS
    cum = jnp.where(fwd, incl, suf)
    tot = jnp.where(fwd, incl[q - 1:q, :], suf[0:1, :])
    w = dt * jnp.exp2(tot - cum)
    lo = lane < DT_W
    dt_sum = dt + pltpu.roll(dt, LANES - HEADS, axis=1)
    rowp = jnp.where(lo, cum - jnp.log2(dt), pltpu.roll(jnp.log2(dt_sum), DT_W, axis=1))
    vals = jnp.where(lo, jnp.exp2(cum), pltpu.roll(w, DT_W, axis=1))
    return cum, rowp, vals


def _expand(vals, wanted, e_ref):
    v1 = vals.astype(BF16).astype(F32)
    r1 = vals - v1
    v2 = r1.astype(BF16).astype(F32)
    terms = (v1, v2, r1 - v2)
    packed = []
    for which, rows in wanted:
        n = rows.stop - rows.start
        lane = lax.broadcasted_iota(jnp.int32, (n, LANES), 1)
        p = jnp.zeros((n, LANES), F32)
        for k in (2, 1, 0):
            t = terms[k][rows]
            shift = (HEADS * (k - which)) % LANES
            t = t if shift == 0 else pltpu.roll(t, shift, axis=1)
            p = jnp.where(lane < HEADS * (k + 1), t, p)
        packed.append(p.astype(BF16))
    stacked = packed[0] if len(packed) == 1 else jnp.concatenate(packed, axis=0)
    return jnp.dot(stacked, e_ref[...], preferred_element_type=F32)


def _state_outer(b_g, xw_g):
    return lax.dot_general(b_g, xw_g, (((0,), (0,)), ((), ())), preferred_element_type=F32)


def _bwd_kernel(x_ref, b_ref, dt_ref, bias_ref, a_ref, e_ref, h0_ref, hs_ref, hfin_ref, st_ref, *, ck):
    s = pl.program_id(1)

    @pl.when(s == 0)
    def _():
        st_ref[...] = h0_ref[0]

    q = CHUNK
    for ci in reversed(range(ck)):
        tok = slice(ci * q, (ci + 1) * q)
        _, _, vals = _chunk_prep(dt_ref[tok, :], bias_ref[...], a_ref[...])
        ex = _expand(vals, [(3, slice(0, q)), (1, slice(0, 16))], e_ref)
        hs_ref[0, ci] = st_ref[...].astype(BF16)
        for g in range(GROUPS):
            rows = slice(g * STATE, (g + 1) * STATE)
            cols = slice(g * GROUP_W, (g + 1) * GROUP_W)
            xw = (x_ref[tok, cols].astype(F32) * ex[0:q, cols]).astype(BF16)
            dec = ex[q:q + 1, cols]
            st_ref[rows, :] = st_ref[rows, :] * dec + _state_outer(b_ref[tok, rows], xw)

    @pl.when(s == pl.num_programs(1) - 1)
    def _():
        hfin_ref[0] = st_ref[...]


def _bwd_scan(zx, dt, bias, a_neg, e_mat, h0, *, n_seq, seq_len, ck):
    ns = seq_len // (ck * CHUNK)
    nc = seq_len // CHUNK
    tm = ck * CHUNK
    tok = lambda b, s: b * ns + (ns - 1 - s)
    return pl.pallas_call(
        functools.partial(_bwd_kernel, ck=ck),
        grid=(n_seq, ns),
        in_specs=[
            pl.BlockSpec((tm, SSD_W), lambda b, s: (tok(b, s), 0)),
            pl.BlockSpec((tm, GN), lambda b, s: (tok(b, s), SSD_W // GN)),
            pl.BlockSpec((tm, LANES), lambda b, s: (tok(b, s), 0)),
            pl.BlockSpec((1, LANES), lambda b, s: (0, 0)),
            pl.BlockSpec((1, LANES), lambda b, s: (0, 0)),
            pl.BlockSpec((LANES, SSD_W), lambda b, s: (0, 0)),
            pl.BlockSpec((1, GN, GROUP_W), lambda b, s: (b, 0, 0)),
        ],
        out_specs=[
            pl.BlockSpec((1, ck, GN, GROUP_W), lambda b, s: (b, ns - 1 - s, 0, 0)),
            pl.BlockSpec((1, GN, GROUP_W), lambda b, s: (b, 0, 0)),
        ],
        out_shape=[
            jax.ShapeDtypeStruct((n_seq, nc, GN, GROUP_W), BF16),
            jax.ShapeDtypeStruct((n_seq, GN, GROUP_W), F32),
        ],
        scratch_shapes=[pltpu.VMEM((GN, GROUP_W), F32)],
        compiler_params=pltpu.CompilerParams(
            dimension_semantics=("parallel", "arbitrary"), vmem_limit_bytes=VMEM_LIMIT),
        name="ssd_bwd_scan",
    )(zx, zx, dt, bias, a_neg, e_mat, h0)


def _ssd_kernel(x_ref, b_ref, c_ref, zs_ref, dt_ref, bias_ref, a_ref, e_ref, h0_ref, hb_ref,
                dskip_ref, ya_ref, hfin_ref, st_ref, y_ref, *, ck):
    s = pl.program_id(1)
    q = CHUNK

    @pl.when(s == 0)
    def _():
        st_ref[...] = h0_ref[0]

    row = lax.broadcasted_iota(jnp.int32, (q, q), 0)
    col = lax.broadcasted_iota(jnp.int32, (q, q), 1)
    lower = col < row
    upper = col > row
    first_head = lax.broadcasted_iota(jnp.int32, (q, 2 * HEAD_DIM), 1) < HEAD_DIM

    for ci in range(ck):
        tok = slice(ci * q, (ci + 1) * q)
        cum, rowp, vals = _chunk_prep(dt_ref[tok, :], bias_ref[...], a_ref[...])
        row_t = rowp.T
        ex = _expand(vals, [(0, slice(0, q)), (1, slice(0, q)), (2, slice(0, q))], e_ref)

        for g in range(GROUPS):
            rows = slice(g * STATE, (g + 1) * STATE)
            cols = slice(g * GROUP_W, (g + 1) * GROUP_W)
            x_g = x_ref[tok, cols]
            xf_g = x_g.astype(F32)
            b_g = b_ref[tok, rows]
            c_g = c_ref[tok, rows]
            cb = lax.dot_general(c_g, b_g, (((1,), (1,)), ((), ())), preferred_element_type=F32)
            ms = []
            for r in range(HEADS_PER_GROUP):
                h = g * HEADS_PER_GROUP + r
                cum_f = jnp.broadcast_to(cum[:, h:h + 1], (q, q))
                cum_b = jnp.broadcast_to(cum[:, HEADS + h:HEADS + h + 1], (q, q))
                arg = jnp.where(lower, cum_f - row_t[h:h + 1, :],
                                jnp.where(upper, cum_b - row_t[HEADS + h:HEADS + h + 1, :],
                                          row_t[2 * HEADS + h:2 * HEADS + h + 1, :]))
                ms.append((cb * jnp.exp2(arg)).astype(BF16))
            ys = []
            for pr in range(HEADS_PER_GROUP // 2):
                xp = x_g[:, pr * 2 * HEAD_DIM:(pr + 1) * 2 * HEAD_DIM]
                zero = jnp.zeros_like(xp)
                rhs = jnp.concatenate([jnp.where(first_head, xp, zero), jnp.where(first_head, zero, xp)], axis=0)
                lhs = jnp.concatenate([ms[2 * pr], ms[2 * pr + 1]], axis=1)
                ys.append(jnp.dot(lhs, rhs, preferred_element_type=F32))
            y_diag = jnp.concatenate(ys, axis=1)
            e_f = ex[0:q, cols]
            y_off = (jnp.dot(c_g, st_ref[rows, :].astype(BF16), preferred_element_type=F32) * e_f
                     + jnp.dot(c_g, hb_ref[0, ci, rows, :], preferred_element_type=F32) * ex[q:2 * q, cols])
            y_ref[tok, cols] = y_diag + y_off + dskip_ref[:, cols] * xf_g
            xw = (xf_g * ex[2 * q:3 * q, cols]).astype(BF16)
            st_ref[rows, :] = st_ref[rows, :] * e_f[q - 1:q, :] + _state_outer(b_g, xw)

        gated = y_ref[tok, :] * _silu(zs_ref[tok, :].astype(F32))
        ya_ref[tok, :] = _rms(gated).astype(BF16)

    @pl.when(s == pl.num_programs(1) - 1)
    def _():
        hfin_ref[0] = st_ref[...]


def _ssd(zx, zr, dt, bias, a_neg, e_mat, h0f, hb_starts, dskip, *, n_seq, seq_len, ck):
    ns = seq_len // (ck * CHUNK)
    tm = ck * CHUNK
    tok = lambda b, s: b * ns + s
    tokens = n_seq * seq_len
    return pl.pallas_call(
        functools.partial(_ssd_kernel, ck=ck),
        grid=(n_seq, ns),
        in_specs=[
            pl.BlockSpec((tm, SSD_W), lambda b, s: (tok(b, s), 0)),
            pl.BlockSpec((tm, GN), lambda b, s: (tok(b, s), SSD_W // GN)),
            pl.BlockSpec((tm, GN), lambda b, s: (tok(b, s), SSD_W // GN + 1)),
            pl.BlockSpec((tm, SSD_W), lambda b, s: (tok(b, s), 0)),
            pl.BlockSpec((tm, LANES), lambda b, s: (tok(b, s), 0)),
            pl.BlockSpec((1, LANES), lambda b, s: (0, 0)),
            pl.BlockSpec((1, LANES), lambda b, s: (0, 0)),
            pl.BlockSpec((LANES, SSD_W), lambda b, s: (0, 0)),
            pl.BlockSpec((1, GN, GROUP_W), lambda b, s: (b, 0, 0)),
            pl.BlockSpec((1, ck, GN, GROUP_W), lambda b, s: (b, s, 0, 0)),
            pl.BlockSpec((1, SSD_W), lambda b, s: (0, 0)),
        ],
        out_specs=[
            pl.BlockSpec((tm, SSD_W), lambda b, s: (tok(b, s), 0)),
            pl.BlockSpec((1, GN, GROUP_W), lambda b, s: (b, 0, 0)),
        ],
        out_shape=[
            jax.ShapeDtypeStruct((tokens, SSD_W), BF16),
            jax.ShapeDtypeStruct((n_seq, GN, GROUP_W), F32),
        ],
        scratch_shapes=[pltpu.VMEM((GN, GROUP_W), F32), pltpu.VMEM((tm, SSD_W), F32)],
        compiler_params=pltpu.CompilerParams(
            dimension_semantics=("parallel", "arbitrary"), vmem_limit_bytes=VMEM_LIMIT),
        name="ssd_main",
    )(zx, zx, zx, zr, dt, bias, a_neg, e_mat, h0f, hb_starts, dskip)


def _out_kernel(ya_ref, u_ref, v_ref, zm_ref, x_ref, gate_ref, gv_ref,
                ws_ref, bs_ref, wout_ref, o_ref, sg_ref):
    tm = x_ref.shape[0]
    vn = (_rms(v_ref[...].astype(F32)) * gv_ref[...]).astype(BF16)
    for c in range(tm // MLP_CHUNK):
        rows = slice(c * MLP_CHUNK, (c + 1) * MLP_CHUNK)
        for g in range(MLP_GROUPS):
            cols = slice(g * MLP_GROUP_DIM, (g + 1) * MLP_GROUP_DIM)
            sg_ref[rows, cols] = jnp.dot(ws_ref[g], vn[rows, cols], preferred_element_type=F32) + bs_ref[:, cols]
    yb = _rms(u_ref[...].astype(F32) * sg_ref[...] * _silu(zm_ref[...].astype(F32))).astype(BF16)
    mixed = (jnp.dot(ya_ref[...], wout_ref[0:SSD_W, :], preferred_element_type=F32)
             + jnp.dot(yb, wout_ref[SSD_W:, :], preferred_element_type=F32))
    o_ref[...] = x_ref[...] + gate_ref[0] * _rms(mixed)


def _out(ya, zr, x2, gate, g_v, bs_mat, wts, layer, *, seq_len, tm):
    tokens, d = x2.shape
    assert gate.shape[0] == 1 or seq_len % tm == 0
    mod_row = (lambda i: 0) if gate.shape[0] == 1 else (lambda i: (i * tm) // seq_len)
    col0 = SSD_W // MLP_W
    const2 = lambda i: (0, 0)
    return pl.pallas_call(
        _out_kernel,
        grid=(tokens // tm,),
        in_specs=[
            pl.BlockSpec((tm, SSD_W), lambda i: (i, 0)),
            pl.BlockSpec((tm, MLP_W), lambda i: (i, col0)),
            pl.BlockSpec((tm, MLP_W), lambda i: (i, col0 + 1)),
            pl.BlockSpec((tm, MLP_W), lambda i: (i, col0 + 2)),
            pl.BlockSpec((tm, d), lambda i: (i, 0)),
            pl.BlockSpec((1, 1, d), lambda i: (mod_row(i), 0, 0)),
            pl.BlockSpec((1, MLP_W), const2),
            pl.BlockSpec((None, MLP_GROUPS, MLP_CHUNK, MLP_CHUNK), lambda i: (layer, 0, 0, 0)),
            pl.BlockSpec((MLP_CHUNK, MLP_W), const2),
            pl.BlockSpec((None, SSD_W + MLP_W, d), lambda i: (layer, 0, 0), pipeline_mode=pl.Buffered(1)),
        ],
        out_specs=pl.BlockSpec((tm, d), lambda i: (i, 0)),
        out_shape=jax.ShapeDtypeStruct((tokens, d), F32),
        scratch_shapes=[pltpu.VMEM((tm, MLP_W), F32)],
        compiler_params=pltpu.CompilerParams(
            dimension_semantics=("parallel",), vmem_limit_bytes=VMEM_LIMIT),
        name="mix_out",
    )(ya, zr, zr, zr, x2, gate, g_v, wts["w_s"], bs_mat, wts["w_out"])


def _expand_matrix():
    j = np.arange(LANES)[:, None]
    h = np.arange(SSD_W)[None, :] // HEAD_DIM
    return jnp.asarray((j % HEADS == h) & (j < 3 * HEADS), dtype=BF16)


def _stream_layer(x2, wts, p, layer, mod_rows, h0f, h0b, *, n_seq, seq_len, row_len, tm_in, tm_out):
    d = x2.shape[1]
    shift = mod_rows[:, None, 0:d]
    gs = p["g_pre"] * (1.0 + mod_rows[:, None, d:2 * d])
    gate = mod_rows[:, None, 2 * d:3 * d] * p["g_post"]
    zx, zr, dt = _inproj(x2, gs, shift, wts, layer, seq_len=seq_len, row_len=row_len, tm=tm_in)
    ck = min(SCAN_CHUNKS, seq_len // CHUNK)
    hb_starts, hb_fin = _bwd_scan(zx, dt, p["bias"], p["a_neg"], p["e_mat"], h0b,
                                  n_seq=n_seq, seq_len=seq_len, ck=ck)
    ya, hf_fin = _ssd(zx, zr, dt, p["bias"], p["a_neg"], p["e_mat"], h0f, hb_starts,
                      p["dskip"], n_seq=n_seq, seq_len=seq_len, ck=ck)
    x_new = _out(ya, zr, x2, gate, p["g_v"], p["bs_mat"], wts, layer,
                 seq_len=seq_len, tm=tm_out)
    return x_new, hf_fin, hb_fin


def _all_layer_weights(w_in, conv_w, conv_b, g_ssd, g_mlp, w_s, w_out):
    o = XBC_W + DT_W
    out_gain = jnp.concatenate([g_ssd, g_mlp], axis=1)[:, :, None]
    depth, d, _ = w_in.shape
    return {
        "w_in": jnp.concatenate([w_in[:, :, :XBC_W].astype(BF16), w_in[:, :, o:].astype(BF16),
                                 w_in[:, :, XBC_W:o].astype(BF16),
                                 jnp.zeros((depth, d, LANES - DT_W), BF16)], axis=2),
        "cw8": jnp.pad(conv_w, ((0, 0), (0, 8 - CONV_W), (0, 0))),
        "cb": conv_b[:, None, :],
        "w_s": w_s.astype(BF16),
        "w_out": (w_out * out_gain).astype(BF16),
    }


def _layer_params(l, g_pre, g_post, dt_bias, a_log, d_skip, g_v, b_s, e_mat):
    pad = jnp.zeros((LANES - DT_W,), F32)
    return {
        "g_pre": g_pre[l][None, None, :],
        "g_post": g_post[l][None, :],
        "bias": jnp.concatenate([dt_bias[l].reshape(-1), pad])[None, :],
        "a_neg": jnp.concatenate([-jnp.exp(a_log[l].reshape(-1)), pad])[None, :],
        "dskip": jnp.repeat(d_skip[l], HEAD_DIM)[None, :],
        "g_v": g_v[l][None, :],
        "bs_mat": jnp.repeat(b_s[l].T, MLP_GROUP_DIM, axis=1),
        "e_mat": e_mat,
    }


def kernel(x, c, ctx, c_ctx, w_ada, b_ada, g_pre, g_post, w_in, conv_w, conv_b, dt_bias, a_log,
           d_skip, g_ssd, g_v, w_s, b_s, g_mlp, w_out):
    bsz, seq, d = x.shape
    ctx_len = ctx.shape[1]
    depth = w_in.shape[0]
    assert seq % LATENT_TM_IN == 0 and ctx_len % CHUNK == 0 and ctx_len & (ctx_len - 1) == 0
    assert w_in.shape[2] == MAIN_W + DT_W and bsz + 1 <= 8

    cc = jnp.concatenate([c, c_ctx[None, :], jnp.zeros((8 - bsz - 1, d), F32)], axis=0)
    mod = _modulation(cc, w_ada, b_ada)
    e_mat = _expand_matrix()
    wts = _all_layer_weights(w_in, conv_w, conv_b, g_ssd, g_mlp, w_s, w_out)

    xs = x.reshape(bsz * seq, d)
    cs = ctx.reshape(bsz * ctx_len, d)
    zeros_h = jnp.zeros((bsz, GN, GROUP_W), F32)
    for l in range(depth):
        p = _layer_params(l, g_pre, g_post, dt_bias, a_log, d_skip, g_v, b_s, e_mat)
        cs, h_f, h_b = _stream_layer(cs, wts, p, l, mod[l, bsz:bsz + 1], zeros_h, zeros_h, n_seq=bsz,
                                     seq_len=ctx_len, row_len=ctx_len, tm_in=bsz * ctx_len, tm_out=ctx_len)
        xs, _, _ = _stream_layer(xs, wts, p, l, mod[l, :bsz], h_f, h_b, n_seq=bsz, seq_len=seq,
                                 row_len=GRID_W, tm_in=LATENT_TM_IN, tm_out=LATENT_TM_OUT)
    return xs.reshape(bsz, seq, d)
```

```python
import functools

import jax
import jax.numpy as jnp
import numpy as np
from jax import lax
from jax.experimental import pallas as pl
from jax.experimental.pallas import tpu as pltpu

F32 = jnp.float32
BF16 = jnp.bfloat16

GRID_W = 64
HEADS = 32
HEAD_DIM = 64
GROUPS = 8
HEADS_PER_GROUP = HEADS // GROUPS
STATE = 128
CHUNK = 128
CONV_W = 5
CONV_PAD = CONV_W // 2
MLP_GROUPS = 16
MLP_GROUP_DIM = 128
MLP_CHUNK = 128
EPS = 1e-6
LOG2E = 1.4426950408889634

SSD_W = HEADS * HEAD_DIM
GN = GROUPS * STATE
XBC_W = SSD_W + 2 * GN
DT_W = 2 * HEADS
MLP_W = MLP_GROUPS * MLP_GROUP_DIM
GROUP_W = HEADS_PER_GROUP * HEAD_DIM
REST_W = SSD_W + 3 * MLP_W
MAIN_W = XBC_W + REST_W
INPROJ_STEPS = 8
CONV_SLAB = XBC_W // INPROJ_STEPS
PLAIN_SLAB = REST_W // INPROJ_STEPS
LANES = 128
SCAN_CHUNKS = 4
LATENT_TM_IN = 1024
LATENT_TM_OUT = 256

VMEM_LIMIT = 56 * 1024 * 1024


def _silu(v):
    return v * (1.0 / (1.0 + jnp.exp(-v)))


def _rms(v):
    return v * lax.rsqrt(jnp.mean(v * v, axis=-1, keepdims=True) + EPS)


def _split3(v):
    v1 = v.astype(BF16)
    r1 = v - v1.astype(F32)
    v2 = r1.astype(BF16)
    r2 = r1 - v2.astype(F32)
    return v1, v2, r2.astype(BF16)


def _mod_kernel(cc_ref, w_ref, b_ref, o_ref):
    sc = _silu(cc_ref[...])
    o_ref[0] = jnp.dot(sc, w_ref[0], preferred_element_type=F32) + b_ref[0]


def _modulation(cc, w_ada, b_ada):
    depth, d, d3 = w_ada.shape
    tn = 768
    return pl.pallas_call(
        _mod_kernel,
        grid=(depth, d3 // tn),
        in_specs=[
            pl.BlockSpec((8, d), lambda l, j: (0, 0)),
            pl.BlockSpec((1, d, tn), lambda l, j: (l, 0, j)),
            pl.BlockSpec((1, 1, tn), lambda l, j: (l, 0, j)),
        ],
        out_specs=pl.BlockSpec((1, 8, tn), lambda l, j: (l, 0, j)),
        out_shape=jax.ShapeDtypeStruct((depth, 8, d3), F32),
        compiler_params=pltpu.CompilerParams(
            dimension_semantics=("parallel", "parallel"), vmem_limit_bytes=VMEM_LIMIT),
        name="adaln_mod",
    )(cc, w_ada, b_ada.reshape(depth, 1, d3))


def _inproj_kernel(x_ref, gs_ref, sh_ref, wc_ref, wp_ref, wdt_ref, cw_ref, cb_ref,
                   zx_ref, zr_ref, dt_ref, hx_ref, *, row_len):
    j = pl.program_id(1)
    tm, tc = zx_ref.shape

    @pl.when(j == 0)
    def _():
        hb = (_rms(x_ref[...]) * gs_ref[0] + sh_ref[0]).astype(BF16)
        hx_ref[...] = hb
        dt_ref[...] = jnp.dot(hb, wdt_ref[...], preferred_element_type=F32)

    hx = hx_ref[...]
    acc = jnp.dot(hx, wc_ref[...], preferred_element_type=F32)
    zr_ref[...] = jnp.dot(hx, wp_ref[...], preferred_element_type=F32).astype(BF16)

    pos = lax.broadcasted_iota(jnp.int32, (tm, tc), 0) & (row_len - 1)
    y = cb_ref[...] + cw_ref[CONV_PAD:CONV_PAD + 1, :] * acc
    for d in (-2, -1, 1, 2):
        shifted = pltpu.roll(acc, (-d) % tm, axis=0)
        valid = (pos >= -d) if d < 0 else (pos < row_len - d)
        y = y + cw_ref[CONV_PAD + d:CONV_PAD + d + 1, :] * jnp.where(valid, shifted, 0.0)
    zx_ref[...] = _silu(y).astype(BF16)


def _inproj(x2, gs, shift, wts, layer, *, seq_len, row_len, tm):
    tokens, d = x2.shape
    assert gs.shape[0] == 1 or seq_len % tm == 0
    mod_row = (lambda i: 0) if gs.shape[0] == 1 else (lambda i: (i * tm) // seq_len)
    kern = functools.partial(_inproj_kernel, row_len=row_len)
    return pl.pallas_call(
        kern,
        grid=(tokens // tm, INPROJ_STEPS),
        in_specs=[
            pl.BlockSpec((tm, d), lambda i, j: (i, 0)),
            pl.BlockSpec((1, 1, d), lambda i, j: (mod_row(i), 0, 0)),
            pl.BlockSpec((1, 1, d), lambda i, j: (mod_row(i), 0, 0)),
            pl.BlockSpec((None, d, CONV_SLAB), lambda i, j: (layer, 0, j)),
            pl.BlockSpec((None, d, PLAIN_SLAB), lambda i, j: (layer, 0, XBC_W // PLAIN_SLAB + j)),
            pl.BlockSpec((None, d, LANES), lambda i, j: (layer, 0, MAIN_W // LANES)),
            pl.BlockSpec((None, 8, CONV_SLAB), lambda i, j: (layer, 0, j)),
            pl.BlockSpec((None, 1, CONV_SLAB), lambda i, j: (layer, 0, j)),
        ],
        out_specs=[
            pl.BlockSpec((tm, CONV_SLAB), lambda i, j: (i, j)),
            pl.BlockSpec((tm, PLAIN_SLAB), lambda i, j: (i, j)),
            pl.BlockSpec((tm, LANES), lambda i, j: (i, 0)),
        ],
        out_shape=[
            jax.ShapeDtypeStruct((tokens, XBC_W), BF16),
            jax.ShapeDtypeStruct((tokens, REST_W), BF16),
            jax.ShapeDtypeStruct((tokens, LANES), F32),
        ],
        scratch_shapes=[pltpu.VMEM((tm, d), BF16)],
        compiler_params=pltpu.CompilerParams(
            dimension_semantics=("parallel", "arbitrary"), vmem_limit_bytes=VMEM_LIMIT),
        name="inproj",
    )(x2, gs, shift, wts["w_in"], wts["w_in"], wts["w_in"], wts["cw8"], wts["cb"])


def _chunk_prep(dtr, bias, a_neg):
    q = dtr.shape[0]
    xx = dtr + bias
    dt = jnp.maximum(xx, 0.0) + jnp.log1p(jnp.exp(-jnp.abs(xx)))
    a = dt * a_neg * LOG2E
    acat = jnp.concatenate(_split3(a), axis=0)
    row = lax.broadcasted_iota(jnp.int32, (q, q), 0)
    col = lax.broadcasted_iota(jnp.int32, (q, q), 1)
    t_inc = jnp.where(col <= row, 1.0, 0.0).astype(BF16)
    t_suf = jnp.where(col >= row, 1.0, 0.0).astype(BF16)
    incl = jnp.dot(jnp.concatenate([t_inc] * 3, axis=1), acat, preferred_element_type=F32)
    suf = jnp.dot(jnp.concatenate([t_suf] * 3, axis=1), acat, preferred_element_type=F32)
    lane = lax.broadcasted_iota(jnp.int32, (q, LANES), 1)
    fwd = lane < HEADS
    cum = jnp.where(fwd, incl, suf)
    tot = jnp.where(fwd, incl[q - 1:q, :], suf[0:1, :])
    w = dt * jnp.exp2(tot - cum)
    lo = lane < DT_W
    dt_sum = dt + pltpu.roll(dt, LANES - HEADS, axis=1)
    rowp = jnp.where(lo, cum - jnp.log2(dt), pltpu.roll(jnp.log2(dt_sum), DT_W, axis=1))
    vals = jnp.where(lo, jnp.exp2(cum), pltpu.roll(w, DT_W, axis=1))
    return cum, rowp, vals


def _expand(vals, wanted, e_ref):
    v1 = vals.astype(BF16).astype(F32)
    r1 = vals - v1
    v2 = r1.astype(BF16).astype(F32)
    terms = (v1, v2, r1 - v2)
    packed = []
    for which, rows in wanted:
        n = rows.stop - rows.start
        lane = lax.broadcasted_iota(jnp.int32, (n, LANES), 1)
        p = jnp.zeros((n, LANES), F32)
        for k in (2, 1, 0):
            t = terms[k][rows]
            shift = (HEADS * (k - which)) % LANES
            t = t if shift == 0 else pltpu.roll(t, shift, axis=1)
            p = jnp.where(lane < HEADS * (k + 1), t, p)
        packed.append(p.astype(BF16))
    stacked = packed[0] if len(packed) == 1 else jnp.concatenate(packed, axis=0)
    return jnp.dot(stacked, e_ref[...], preferred_element_type=F32)


def _state_outer(b_g, xw_g):
    return lax.dot_general(b_g, xw_g, (((0,), (0,)), ((), ())), preferred_element_type=F32)


def _bwd_kernel(x_ref, b_ref, dt_ref, bias_ref, a_ref, e_ref, h0_ref, hs_ref, hfin_ref, st_ref, *, ck):
    s = pl.program_id(1)

    @pl.when(s == 0)
    def _():
        st_ref[...] = h0_ref[0]

    q = CHUNK
    for ci in reversed(range(ck)):
        tok = slice(ci * q, (ci + 1) * q)
        _, _, vals = _chunk_prep(dt_ref[tok, :], bias_ref[...], a_ref[...])
        ex = _expand(vals, [(3, slice(0, q)), (1, slice(0, 16))], e_ref)
        hs_ref[0, ci] = st_ref[...].astype(BF16)
        for g in range(GROUPS):
            rows = slice(g * STATE, (g + 1) * STATE)
            cols = slice(g * GROUP_W, (g + 1) * GROUP_W)
            xw = (x_ref[tok, cols].astype(F32) * ex[0:q, cols]).astype(BF16)
            dec = ex[q:q + 1, cols]
            st_ref[rows, :] = st_ref[rows, :] * dec + _state_outer(b_ref[tok, rows], xw)

    @pl.when(s == pl.num_programs(1) - 1)
    def _():
        hfin_ref[0] = st_ref[...]


def _bwd_scan(zx, dt, bias, a_neg, e_mat, h0, *, n_seq, seq_len, ck):
    ns = seq_len // (ck * CHUNK)
    nc = seq_len // CHUNK
    tm = ck * CHUNK
    tok = lambda b, s: b * ns + (ns - 1 - s)
    return pl.pallas_call(
        functools.partial(_bwd_kernel, ck=ck),
        grid=(n_seq, ns),
        in_specs=[
            pl.BlockSpec((tm, SSD_W), lambda b, s: (tok(b, s), 0)),
            pl.BlockSpec((tm, GN), lambda b, s: (tok(b, s), SSD_W // GN)),
            pl.BlockSpec((tm, LANES), lambda b, s: (tok(b, s), 0)),
            pl.BlockSpec((1, LANES), lambda b, s: (0, 0)),
            pl.BlockSpec((1, LANES), lambda b, s: (0, 0)),
            pl.BlockSpec((LANES, SSD_W), lambda b, s: (0, 0)),
            pl.BlockSpec((1, GN, GROUP_W), lambda b, s: (b, 0, 0)),
        ],
        out_specs=[
            pl.BlockSpec((1, ck, GN, GROUP_W), lambda b, s: (b, ns - 1 - s, 0, 0)),
            pl.BlockSpec((1, GN, GROUP_W), lambda b, s: (b, 0, 0)),
        ],
        out_shape=[
            jax.ShapeDtypeStruct((n_seq, nc, GN, GROUP_W), BF16),
            jax.ShapeDtypeStruct((n_seq, GN, GROUP_W), F32),
        ],
        scratch_shapes=[pltpu.VMEM((GN, GROUP_W), F32)],
        compiler_params=pltpu.CompilerParams(
            dimension_semantics=("parallel", "arbitrary"), vmem_limit_bytes=VMEM_LIMIT),
        name="ssd_bwd_scan",
    )(zx, zx, dt, bias, a_neg, e_mat, h0)


def _ssd_kernel(x_ref, b_ref, c_ref, zs_ref, dt_ref, bias_ref, a_ref, e_ref, h0_ref, hb_ref,
                dskip_ref, ya_ref, hfin_ref, st_ref, y_ref, *, ck):
    s = pl.program_id(1)
    q = CHUNK

    @pl.when(s == 0)
    def _():
        st_ref[...] = h0_ref[0]

    row = lax.broadcasted_iota(jnp.int32, (q, q), 0)
    col = lax.broadcasted_iota(jnp.int32, (q, q), 1)
    lower = col < row
    upper = col > row
    first_head = lax.broadcasted_iota(jnp.int32, (q, 2 * HEAD_DIM), 1) < HEAD_DIM

    for ci in range(ck):
        tok = slice(ci * q, (ci + 1) * q)
        cum, rowp, vals = _chunk_prep(dt_ref[tok, :], bias_ref[...], a_ref[...])
        row_t = rowp.T
        ex = _expand(vals, [(0, slice(0, q)), (1, slice(0, q)), (2, slice(0, q))], e_ref)

        for g in range(GROUPS):
            rows = slice(g * STATE, (g + 1) * STATE)
            cols = slice(g * GROUP_W, (g + 1) * GROUP_W)
            x_g = x_ref[tok, cols]
            xf_g = x_g.astype(F32)
            b_g = b_ref[tok, rows]
            c_g = c_ref[tok, rows]
            cb = lax.dot_general(c_g, b_g, (((1,), (1,)), ((), ())), preferred_element_type=F32)
            ms = []
            for r in range(HEADS_PER_GROUP):
                h = g * HEADS_PER_GROUP + r
                cum_f = jnp.broadcast_to(cum[:, h:h + 1], (q, q))
                cum_b = jnp.broadcast_to(cum[:, HEADS + h:HEADS + h + 1], (q, q))
                arg = jnp.where(lower, cum_f - row_t[h:h + 1, :],
                                jnp.where(upper, cum_b - row_t[HEADS + h:HEADS + h + 1, :],
                                          row_t[2 * HEADS + h:2 * HEADS + h + 1, :]))
                ms.append((cb * jnp.exp2(arg)).astype(BF16))
            ys = []
            for pr in range(HEADS_PER_GROUP // 2):
                xp = x_g[:, pr * 2 * HEAD_DIM:(pr + 1) * 2 * HEAD_DIM]
                zero = jnp.zeros_like(xp)
                rhs = jnp.concatenate([jnp.where(first_head, xp, zero), jnp.where(first_head, zero, xp)], axis=0)
                lhs = jnp.concatenate([ms[2 * pr], ms[2 * pr + 1]], axis=1)
                ys.append(jnp.dot(lhs, rhs, preferred_element_type=F32))
            y_diag = jnp.concatenate(ys, axis=1)
            e_f = ex[0:q, cols]
            y_off = (jnp.dot(c_g, st_ref[rows, :].astype(BF16), preferred_element_type=F32) * e_f
                     + jnp.dot(c_g, hb_ref[0, ci, rows, :], preferred_element_type=F32) * ex[q:2 * q, cols])
            y_ref[tok, cols] = y_diag + y_off + dskip_ref[:, cols] * xf_g
            xw = (xf_g * ex[2 * q:3 * q, cols]).astype(BF16)
            st_ref[rows, :] = st_ref[rows, :] * e_f[q - 1:q, :] + _state_outer(b_g, xw)

        gated = y_ref[tok, :] * _silu(zs_ref[tok, :].astype(F32))
        ya_ref[tok, :] = _rms(gated).astype(BF16)

    @pl.when(s == pl.num_programs(1) - 1)
    def _():
        hfin_ref[0] = st_ref[...]


def _ssd(zx, zr, dt, bias, a_neg, e_mat, h0f, hb_starts, dskip, *, n_seq, seq_len, ck):
    ns = seq_len // (ck * CHUNK)
    tm = ck * CHUNK
    tok = lambda b, s: b * ns + s
    tokens = n_seq * seq_len
    return pl.pallas_call(
        functools.partial(_ssd_kernel, ck=ck),
        grid=(n_seq, ns),
        in_specs=[
            pl.BlockSpec((tm, SSD_W), lambda b, s: (tok(b, s), 0)),
            pl.BlockSpec((tm, GN), lambda b, s: (tok(b, s), SSD_W // GN)),
            pl.BlockSpec((tm, GN), lambda b, s: (tok(b, s), SSD_W // GN + 1)),
            pl.BlockSpec((tm, SSD_W), lambda b, s: (tok(b, s), 0)),
            pl.BlockSpec((tm, LANES), lambda b, s: (tok(b, s), 0)),
            pl.BlockSpec((1, LANES), lambda b, s: (0, 0)),
            pl.BlockSpec((1, LANES), lambda b, s: (0, 0)),
            pl.BlockSpec((LANES, SSD_W), lambda b, s: (0, 0)),
            pl.BlockSpec((1, GN, GROUP_W), lambda b, s: (b, 0, 0)),
            pl.BlockSpec((1, ck, GN, GROUP_W), lambda b, s: (b, s, 0, 0)),
            pl.BlockSpec((1, SSD_W), lambda b, s: (0, 0)),
        ],
        out_specs=[
            pl.BlockSpec((tm, SSD_W), lambda b, s: (tok(b, s), 0)),
            pl.BlockSpec((1, GN, GROUP_W), lambda b, s: (b, 0, 0)),
        ],
        out_shape=[
            jax.ShapeDtypeStruct((tokens, SSD_W), BF16),
            jax.ShapeDtypeStruct((n_seq, GN, GROUP_W), F32),
        ],
        scratch_shapes=[pltpu.VMEM((GN, GROUP_W), F32), pltpu.VMEM((tm, SSD_W), F32)],
        compiler_params=pltpu.CompilerParams(
            dimension_semantics=("parallel", "arbitrary"), vmem_limit_bytes=VMEM_LIMIT),
        name="ssd_main",
    )(zx, zx, zx, zr, dt, bias, a_neg, e_mat, h0f, hb_starts, dskip)


def _out_kernel(ya_ref, u_ref, v_ref, zm_ref, x_ref, gate_ref, gv_ref,
                ws_ref, bs_ref, wout_ref, o_ref, sg_ref):
    tm = x_ref.shape[0]
    vn = (_rms(v_ref[...].astype(F32)) * gv_ref[...]).astype(BF16)
    for c in range(tm // MLP_CHUNK):
        rows = slice(c * MLP_CHUNK, (c + 1) * MLP_CHUNK)
        for g in range(MLP_GROUPS):
            cols = slice(g * MLP_GROUP_DIM, (g + 1) * MLP_GROUP_DIM)
            sg_ref[rows, cols] = jnp.dot(ws_ref[g], vn[rows, cols], preferred_element_type=F32) + bs_ref[:, cols]
    yb = _rms(u_ref[...].astype(F32) * sg_ref[...] * _silu(zm_ref[...].astype(F32))).astype(BF16)
    mixed = (jnp.dot(ya_ref[...], wout_ref[0:SSD_W, :], preferred_element_type=F32)
             + jnp.dot(yb, wout_ref[SSD_W:, :], preferred_element_type=F32))
    o_ref[...] = x_ref[...] + gate_ref[0] * _rms(mixed)


def _out(ya, zr, x2, gate, g_v, bs_mat, wts, layer, *, seq_len, tm):
    tokens, d = x2.shape
    assert gate.shape[0] == 1 or seq_len % tm == 0
    mod_row = (lambda i: 0) if gate.shape[0] == 1 else (lambda i: (i * tm) // seq_len)
    col0 = SSD_W // MLP_W
    const2 = lambda i: (0, 0)
    return pl.pallas_call(
        _out_kernel,
        grid=(tokens // tm,),
        in_specs=[
            pl.BlockSpec((tm, SSD_W), lambda i: (i, 0)),
            pl.BlockSpec((tm, MLP_W), lambda i: (i, col0)),
            pl.BlockSpec((tm, MLP_W), lambda i: (i, col0 + 1)),
            pl.BlockSpec((tm, MLP_W), lambda i: (i, col0 + 2)),
            pl.BlockSpec((tm, d), lambda i: (i, 0)),
            pl.BlockSpec((1, 1, d), lambda i: (mod_row(i), 0, 0)),
            pl.BlockSpec((1, MLP_W), const2),
            pl.BlockSpec((None, MLP_GROUPS, MLP_CHUNK, MLP_CHUNK), lambda i: (layer, 0, 0, 0)),
            pl.BlockSpec((MLP_CHUNK, MLP_W), const2),
            pl.BlockSpec((None, SSD_W + MLP_W, d), lambda i: (layer, 0, 0), pipeline_mode=pl.Buffered(1)),
        ],
        out_specs=pl.BlockSpec((tm, d), lambda i: (i, 0)),
        out_shape=jax.ShapeDtypeStruct((tokens, d), F32),
        scratch_shapes=[pltpu.VMEM((tm, MLP_W), F32)],
        compiler_params=pltpu.CompilerParams(
            dimension_semantics=("parallel",), vmem_limit_bytes=VMEM_LIMIT),
        name="mix_out",
    )(ya, zr, zr, zr, x2, gate, g_v, wts["w_s"], bs_mat, wts["w_out"])


def _expand_matrix():
    j = np.arange(LANES)[:, None]
    h = np.arange(SSD_W)[None, :] // HEAD_DIM
    return jnp.asarray((j % HEADS == h) & (j < 3 * HEADS), dtype=BF16)


def _stream_layer(x2, wts, p, layer, mod_rows, h0f, h0b, *, n_seq, seq_len, row_len, tm_in, tm_out):
    d = x2.shape[1]
    shift = mod_rows[:, None, 0:d]
    gs = p["g_pre"] * (1.0 + mod_rows[:, None, d:2 * d])
    gate = mod_rows[:, None, 2 * d:3 * d] * p["g_post"]
    zx, zr, dt = _inproj(x2, gs, shift, wts, layer, seq_len=seq_len, row_len=row_len, tm=tm_in)
    ck = min(SCAN_CHUNKS, seq_len // CHUNK)
    hb_starts, hb_fin = _bwd_scan(zx, dt, p["bias"], p["a_neg"], p["e_mat"], h0b,
                                  n_seq=n_seq, seq_len=seq_len, ck=ck)
    ya, hf_fin = _ssd(zx, zr, dt, p["bias"], p["a_neg"], p["e_mat"], h0f, hb_starts,
                      p["dskip"], n_seq=n_seq, seq_len=seq_len, ck=ck)
    x_new = _out(ya, zr, x2, gate, p["g_v"], p["bs_mat"], wts, layer,
                 seq_len=seq_len, tm=tm_out)
    return x_new, hf_fin, hb_fin


def _regroup_kernel(w_ref, o_ref):
    o = XBC_W + DT_W
    rows = w_ref.shape[0]
    o_ref[:, 0:XBC_W] = w_ref[:, 0:XBC_W].astype(BF16)
    o_ref[:, XBC_W:MAIN_W] = w_ref[:, o:].astype(BF16)
    o_ref[:, MAIN_W:] = jnp.concatenate(
        [w_ref[:, XBC_W:o], jnp.zeros((rows, LANES - DT_W), F32)], axis=1).astype(BF16)


def _regroup_w_in(w_in):
    depth, d, n = w_in.shape
    tr = 256
    return pl.pallas_call(
        _regroup_kernel,
        grid=(depth, d // tr),
        in_specs=[pl.BlockSpec((None, tr, n), lambda l, i: (l, i, 0))],
        out_specs=pl.BlockSpec((None, tr, MAIN_W + LANES), lambda l, i: (l, i, 0)),
        out_shape=jax.ShapeDtypeStruct((depth, d, MAIN_W + LANES), BF16),
        compiler_params=pltpu.CompilerParams(
            dimension_semantics=("parallel", "parallel"), vmem_limit_bytes=VMEM_LIMIT),
        name="regroup_w_in",
    )(w_in)


def _all_layer_weights(w_in, conv_w, conv_b, g_ssd, g_mlp, w_s, w_out):
    o = XBC_W + DT_W
    out_gain = jnp.concatenate([g_ssd, g_mlp], axis=1)[:, :, None]
    return {
        "w_in": _regroup_w_in(w_in),
        "cw8": jnp.pad(conv_w, ((0, 0), (0, 8 - CONV_W), (0, 0))),
        "cb": conv_b[:, None, :],
        "w_s": w_s.astype(BF16),
        "w_out": (w_out * out_gain).astype(BF16),
    }


def _layer_params(l, g_pre, g_post, dt_bias, a_log, d_skip, g_v, b_s, e_mat):
    pad = jnp.zeros((LANES - DT_W,), F32)
    return {
        "g_pre": g_pre[l][None, None, :],
        "g_post": g_post[l][None, :],
        "bias": jnp.concatenate([dt_bias[l].reshape(-1), pad])[None, :],
        "a_neg": jnp.concatenate([-jnp.exp(a_log[l].reshape(-1)), pad])[None, :],
        "dskip": jnp.repeat(d_skip[l], HEAD_DIM)[None, :],
        "g_v": g_v[l][None, :],
        "bs_mat": jnp.repeat(b_s[l].T, MLP_GROUP_DIM, axis=1),
        "e_mat": e_mat,
    }


def kernel(x, c, ctx, c_ctx, w_ada, b_ada, g_pre, g_post, w_in, conv_w, conv_b, dt_bias, a_log,
           d_skip, g_ssd, g_v, w_s, b_s, g_mlp, w_out):
    bsz, seq, d = x.shape
    ctx_len = ctx.shape[1]
    depth = w_in.shape[0]
    assert seq % LATENT_TM_IN == 0 and ctx_len % CHUNK == 0 and ctx_len & (ctx_len - 1) == 0
    assert w_in.shape[2] == MAIN_W + DT_W and bsz + 1 <= 8

    cc = jnp.concatenate([c, c_ctx[None, :], jnp.zeros((8 - bsz - 1, d), F32)], axis=0)
    mod = _modulation(cc, w_ada, b_ada)
    e_mat = _expand_matrix()
    wts = _all_layer_weights(w_in, conv_w, conv_b, g_ssd, g_mlp, w_s, w_out)

    xs = x.reshape(bsz * seq, d)
    cs = ctx.reshape(bsz * ctx_len, d)
    zeros_h = jnp.zeros((bsz, GN, GROUP_W), F32)
    for l in range(depth):
        p = _layer_params(l, g_pre, g_post, dt_bias, a_log, d_skip, g_v, b_s, e_mat)
        cs, h_f, h_b = _stream_layer(cs, wts, p, l, mod[l, bsz:bsz + 1], zeros_h, zeros_h, n_seq=bsz,
                                     seq_len=ctx_len, row_len=ctx_len, tm_in=bsz * ctx_len, tm_out=ctx_len)
        xs, _, _ = _stream_layer(xs, wts, p, l, mod[l, :bsz], h_f, h_b, n_seq=bsz, seq_len=seq,
                                 row_len=GRID_W, tm_in=LATENT_TM_IN, tm_out=LATENT_TM_OUT)
    return xs.reshape(bsz, seq, d)
```

```python
import functools

import jax
import jax.numpy as jnp
import numpy as np
from jax import lax
from jax.experimental import pallas as pl
from jax.experimental.pallas import tpu as pltpu

F32 = jnp.float32
BF16 = jnp.bfloat16

GRID_W = 64
HEADS = 32
HEAD_DIM = 64
GROUPS = 8
HEADS_PER_GROUP = HEADS // GROUPS
STATE = 128
CHUNK = 128
CONV_W = 5
CONV_PAD = CONV_W // 2
MLP_GROUPS = 16
MLP_GROUP_DIM = 128
MLP_CHUNK = 128
EPS = 1e-6
LOG2E = 1.4426950408889634

SSD_W = HEADS * HEAD_DIM
GN = GROUPS * STATE
XBC_W = SSD_W + 2 * GN
DT_W = 2 * HEADS
MLP_W = MLP_GROUPS * MLP_GROUP_DIM
GROUP_W = HEADS_PER_GROUP * HEAD_DIM
REST_W = SSD_W + 3 * MLP_W
MAIN_W = XBC_W + REST_W
INPROJ_STEPS = 8
CONV_SLAB = XBC_W // INPROJ_STEPS
PLAIN_SLAB = REST_W // INPROJ_STEPS
LANES = 128
SCAN_CHUNKS = 4
LATENT_TM_IN = 1024
LATENT_TM_OUT = 256

VMEM_LIMIT = 56 * 1024 * 1024


def _silu(v):
    h = 0.5 * v
    return h * (1.0 + jnp.tanh(h))


def _rms(v):
    return v * lax.rsqrt(jnp.mean(v * v, axis=-1, keepdims=True) + EPS)


def _split3(v):
    v1 = v.astype(BF16)
    r1 = v - v1.astype(F32)
    v2 = r1.astype(BF16)
    r2 = r1 - v2.astype(F32)
    return v1, v2, r2.astype(BF16)


def _mod_kernel(cc_ref, w_ref, b_ref, o_ref):
    sc = _silu(cc_ref[...])
    o_ref[0] = jnp.dot(sc, w_ref[0], preferred_element_type=F32) + b_ref[0]


def _modulation(cc, w_ada, b_ada):
    depth, d, d3 = w_ada.shape
    tn = 768
    return pl.pallas_call(
        _mod_kernel,
        grid=(depth, d3 // tn),
        in_specs=[
            pl.BlockSpec((8, d), lambda l, j: (0, 0)),
            pl.BlockSpec((1, d, tn), lambda l, j: (l, 0, j)),
            pl.BlockSpec((1, 1, tn), lambda l, j: (l, 0, j)),
        ],
        out_specs=pl.BlockSpec((1, 8, tn), lambda l, j: (l, 0, j)),
        out_shape=jax.ShapeDtypeStruct((depth, 8, d3), F32),
        compiler_params=pltpu.CompilerParams(
            dimension_semantics=("parallel", "parallel"), vmem_limit_bytes=VMEM_LIMIT),
        name="adaln_mod",
    )(cc, w_ada, b_ada.reshape(depth, 1, d3))


def _inproj_kernel(x_ref, gs_ref, sh_ref, wc_ref, wp_ref, wdt_ref, cw_ref, cb_ref,
                   zx_ref, zr_ref, dt_ref, hx_ref, *, row_len):
    j = pl.program_id(1)
    tm, tc = zx_ref.shape

    @pl.when(j == 0)
    def _():
        hb = (_rms(x_ref[...]) * gs_ref[0] + sh_ref[0]).astype(BF16)
        hx_ref[...] = hb
        dt_ref[...] = jnp.dot(hb, wdt_ref[...], preferred_element_type=F32)

    hx = hx_ref[...]
    acc = jnp.dot(hx, wc_ref[...], preferred_element_type=F32)
    zr_ref[...] = jnp.dot(hx, wp_ref[...], preferred_element_type=F32).astype(BF16)

    rows3 = acc.reshape(tm // row_len, row_len, tc)
    y = cb_ref[...] + cw_ref[CONV_PAD:CONV_PAD + 1, :] * acc
    for d in (-2, -1, 1, 2):
        pad = jnp.zeros((tm // row_len, abs(d), tc), F32)
        if d < 0:
            shifted = jnp.concatenate([pad, rows3[:, :row_len + d, :]], axis=1)
        else:
            shifted = jnp.concatenate([rows3[:, d:, :], pad], axis=1)
        y = y + cw_ref[CONV_PAD + d:CONV_PAD + d + 1, :] * shifted.reshape(tm, tc)
    zx_ref[...] = _silu(y).astype(BF16)


def _inproj(x2, gs, shift, wts, layer, *, seq_len, row_len, tm):
    tokens, d = x2.shape
    assert gs.shape[0] == 1 or seq_len % tm == 0
    mod_row = (lambda i: 0) if gs.shape[0] == 1 else (lambda i: (i * tm) // seq_len)
    kern = functools.partial(_inproj_kernel, row_len=row_len)
    return pl.pallas_call(
        kern,
        grid=(tokens // tm, INPROJ_STEPS),
        in_specs=[
            pl.BlockSpec((tm, d), lambda i, j: (i, 0)),
            pl.BlockSpec((1, 1, d), lambda i, j: (mod_row(i), 0, 0)),
            pl.BlockSpec((1, 1, d), lambda i, j: (mod_row(i), 0, 0)),
            pl.BlockSpec((None, d, CONV_SLAB), lambda i, j: (layer, 0, j)),
            pl.BlockSpec((None, d, PLAIN_SLAB), lambda i, j: (layer, 0, j)),
            pl.BlockSpec((None, d, LANES), lambda i, j: (layer, 0, 0)),
            pl.BlockSpec((None, 8, CONV_SLAB), lambda i, j: (layer, 0, j)),
            pl.BlockSpec((None, 1, CONV_SLAB), lambda i, j: (layer, 0, j)),
        ],
        out_specs=[
            pl.BlockSpec((tm, CONV_SLAB), lambda i, j: (i, j)),
            pl.BlockSpec((tm, PLAIN_SLAB), lambda i, j: (i, j)),
            pl.BlockSpec((tm, LANES), lambda i, j: (i, 0)),
        ],
        out_shape=[
            jax.ShapeDtypeStruct((tokens, XBC_W), BF16),
            jax.ShapeDtypeStruct((tokens, REST_W), BF16),
            jax.ShapeDtypeStruct((tokens, LANES), F32),
        ],
        scratch_shapes=[pltpu.VMEM((tm, d), BF16)],
        compiler_params=pltpu.CompilerParams(
            dimension_semantics=("parallel", "arbitrary"), vmem_limit_bytes=VMEM_LIMIT),
        name="inproj",
    )(x2, gs, shift, wts["w_conv"], wts["w_plain"], wts["w_dt"], wts["cw8"], wts["cb"])


def _chunk_prep(dtr, bias, a_neg):
    q = dtr.shape[0]
    xx = dtr + bias
    dt = jnp.maximum(xx, 0.0) + jnp.log1p(jnp.exp(-jnp.abs(xx)))
    a = dt * a_neg * LOG2E
    acat = jnp.concatenate(_split3(a), axis=0)
    row = lax.broadcasted_iota(jnp.int32, (q, q), 0)
    col = lax.broadcasted_iota(jnp.int32, (q, q), 1)
    t_inc = jnp.where(col <= row, 1.0, 0.0).astype(BF16)
    t_suf = jnp.where(col >= row, 1.0, 0.0).astype(BF16)
    incl = jnp.dot(jnp.concatenate([t_inc] * 3, axis=1), acat, preferred_element_type=F32)
    suf = jnp.dot(jnp.concatenate([t_suf] * 3, axis=1), acat, preferred_element_type=F32)
    lane = lax.broadcasted_iota(jnp.int32, (q, LANES), 1)
    fwd = lane < HEADS
    cum = jnp.where(fwd, incl, suf)
    tot = jnp.where(fwd, incl[q - 1:q, :], suf[0:1, :])
    w = dt * jnp.exp2(tot - cum)
    lo = lane < DT_W
    dt_sum = dt + pltpu.roll(dt, LANES - HEADS, axis=1)
    rowp = jnp.where(lo, cum - jnp.log2(dt), pltpu.roll(jnp.log2(dt_sum), DT_W, axis=1))
    vals = jnp.where(lo, jnp.exp2(cum), pltpu.roll(w, DT_W, axis=1))
    return cum, rowp, vals


def _expand(vals, wanted, e_ref):
    v1 = vals.astype(BF16).astype(F32)
    r1 = vals - v1
    v2 = r1.astype(BF16).astype(F32)
    terms = (v1, v2, r1 - v2)
    packed = []
    for which, rows in wanted:
        n = rows.stop - rows.start
        lane = lax.broadcasted_iota(jnp.int32, (n, LANES), 1)
        p = jnp.zeros((n, LANES), F32)
        for k in (2, 1, 0):
            t = terms[k][rows]
            shift = (HEADS * (k - which)) % LANES
            t = t if shift == 0 else pltpu.roll(t, shift, axis=1)
            p = jnp.where(lane < HEADS * (k + 1), t, p)
        packed.append(p.astype(BF16))
    stacked = packed[0] if len(packed) == 1 else jnp.concatenate(packed, axis=0)
    return jnp.dot(stacked, e_ref[...], preferred_element_type=F32)


def _state_outer(b_g, xw_g):
    return lax.dot_general(b_g, xw_g, (((0,), (0,)), ((), ())), preferred_element_type=F32)


def _bwd_kernel(x_ref, b_ref, dt_ref, bias_ref, a_ref, e_ref, h0_ref, hs_ref, hfin_ref, st_ref, *, ck):
    s = pl.program_id(1)

    @pl.when(s == 0)
    def _():
        st_ref[...] = h0_ref[0]

    q = CHUNK
    for ci in reversed(range(ck)):
        tok = slice(ci * q, (ci + 1) * q)
        _, _, vals = _chunk_prep(dt_ref[tok, :], bias_ref[...], a_ref[...])
        ex = _expand(vals, [(3, slice(0, q)), (1, slice(0, 16))], e_ref)
        hs_ref[0, ci] = st_ref[...].astype(BF16)
        for g in range(GROUPS):
            rows = slice(g * STATE, (g + 1) * STATE)
            cols = slice(g * GROUP_W, (g + 1) * GROUP_W)
            xw = (x_ref[tok, cols].astype(F32) * ex[0:q, cols]).astype(BF16)
            dec = ex[q:q + 1, cols]
            st_ref[rows, :] = st_ref[rows, :] * dec + _state_outer(b_ref[tok, rows], xw)

    @pl.when(s == pl.num_programs(1) - 1)
    def _():
        hfin_ref[0] = st_ref[...]


def _bwd_scan(zx, dt, bias, a_neg, e_mat, h0, *, n_seq, seq_len, ck):
    ns = seq_len // (ck * CHUNK)
    nc = seq_len // CHUNK
    tm = ck * CHUNK
    tok = lambda b, s: b * ns + (ns - 1 - s)
    return pl.pallas_call(
        functools.partial(_bwd_kernel, ck=ck),
        grid=(n_seq, ns),
        in_specs=[
            pl.BlockSpec((tm, SSD_W), lambda b, s: (tok(b, s), 0)),
            pl.BlockSpec((tm, GN), lambda b, s: (tok(b, s), SSD_W // GN)),
            pl.BlockSpec((tm, LANES), lambda b, s: (tok(b, s), 0)),
            pl.BlockSpec((1, LANES), lambda b, s: (0, 0)),
            pl.BlockSpec((1, LANES), lambda b, s: (0, 0)),
            pl.BlockSpec((LANES, SSD_W), lambda b, s: (0, 0)),
            pl.BlockSpec((1, GN, GROUP_W), lambda b, s: (b, 0, 0)),
        ],
        out_specs=[
            pl.BlockSpec((1, ck, GN, GROUP_W), lambda b, s: (b, ns - 1 - s, 0, 0)),
            pl.BlockSpec((1, GN, GROUP_W), lambda b, s: (b, 0, 0)),
        ],
        out_shape=[
            jax.ShapeDtypeStruct((n_seq, nc, GN, GROUP_W), BF16),
            jax.ShapeDtypeStruct((n_seq, GN, GROUP_W), F32),
        ],
        scratch_shapes=[pltpu.VMEM((GN, GROUP_W), F32)],
        compiler_params=pltpu.CompilerParams(
            dimension_semantics=("parallel", "arbitrary"), vmem_limit_bytes=VMEM_LIMIT),
        name="ssd_bwd_scan",
    )(zx, zx, dt, bias, a_neg, e_mat, h0)


def _ssd_kernel(x_ref, b_ref, c_ref, zs_ref, dt_ref, bias_ref, a_ref, e_ref, h0_ref, hb_ref,
                dskip_ref, ya_ref, hfin_ref, st_ref, y_ref, *, ck):
    s = pl.program_id(1)
    q = CHUNK

    @pl.when(s == 0)
    def _():
        st_ref[...] = h0_ref[0]

    row = lax.broadcasted_iota(jnp.int32, (q, q), 0)
    col = lax.broadcasted_iota(jnp.int32, (q, q), 1)
    lower = col < row
    upper = col > row
    first_head = lax.broadcasted_iota(jnp.int32, (q, 2 * HEAD_DIM), 1) < HEAD_DIM

    for ci in range(ck):
        tok = slice(ci * q, (ci + 1) * q)
        cum, rowp, vals = _chunk_prep(dt_ref[tok, :], bias_ref[...], a_ref[...])
        row_t = rowp.T
        ex = _expand(vals, [(0, slice(0, q)), (1, slice(0, q)), (2, slice(0, q))], e_ref)

        for g in range(GROUPS):
            rows = slice(g * STATE, (g + 1) * STATE)
            cols = slice(g * GROUP_W, (g + 1) * GROUP_W)
            x_g = x_ref[tok, cols]
            xf_g = x_g.astype(F32)
            b_g = b_ref[tok, rows]
            c_g = c_ref[tok, rows]
            cb = lax.dot_general(c_g, b_g, (((1,), (1,)), ((), ())), preferred_element_type=F32)
            ms = []
            for r in range(HEADS_PER_GROUP):
                h = g * HEADS_PER_GROUP + r
                cum_f = jnp.broadcast_to(cum[:, h:h + 1], (q, q))
                cum_b = jnp.broadcast_to(cum[:, HEADS + h:HEADS + h + 1], (q, q))
                arg = jnp.where(lower, cum_f - row_t[h:h + 1, :],
                                jnp.where(upper, cum_b - row_t[HEADS + h:HEADS + h + 1, :],
                                          row_t[2 * HEADS + h:2 * HEADS + h + 1, :]))
                ms.append((cb * jnp.exp2(arg)).astype(BF16))
            ys = []
            for pr in range(HEADS_PER_GROUP // 2):
                xp = x_g[:, pr * 2 * HEAD_DIM:(pr + 1) * 2 * HEAD_DIM]
                zero = jnp.zeros_like(xp)
                rhs = jnp.concatenate([jnp.where(first_head, xp, zero), jnp.where(first_head, zero, xp)], axis=0)
                lhs = jnp.concatenate([ms[2 * pr], ms[2 * pr + 1]], axis=1)
                ys.append(jnp.dot(lhs, rhs, preferred_element_type=F32))
            y_diag = jnp.concatenate(ys, axis=1)
            e_f = ex[0:q, cols]
            y_off = (jnp.dot(c_g, st_ref[rows, :].astype(BF16), preferred_element_type=F32) * e_f
                     + jnp.dot(c_g, hb_ref[0, ci, rows, :], preferred_element_type=F32) * ex[q:2 * q, cols])
            y_ref[tok, cols] = y_diag + y_off + dskip_ref[:, cols] * xf_g
            xw = (xf_g * ex[2 * q:3 * q, cols]).astype(BF16)
            st_ref[rows, :] = st_ref[rows, :] * e_f[q - 1:q, :] + _state_outer(b_g, xw)

        gated = y_ref[tok, :] * _silu(zs_ref[tok, :].astype(F32))
        ya_ref[tok, :] = _rms(gated).astype(BF16)

    @pl.when(s == pl.num_programs(1) - 1)
    def _():
        hfin_ref[0] = st_ref[...]


def _ssd(zx, zr, dt, bias, a_neg, e_mat, h0f, hb_starts, dskip, *, n_seq, seq_len, ck):
    ns = seq_len // (ck * CHUNK)
    tm = ck * CHUNK
    tok = lambda b, s: b * ns + s
    tokens = n_seq * seq_len
    return pl.pallas_call(
        functools.partial(_ssd_kernel, ck=ck),
        grid=(n_seq, ns),
        in_specs=[
            pl.BlockSpec((tm, SSD_W), lambda b, s: (tok(b, s), 0)),
            pl.BlockSpec((tm, GN), lambda b, s: (tok(b, s), SSD_W // GN)),
            pl.BlockSpec((tm, GN), lambda b, s: (tok(b, s), SSD_W // GN + 1)),
            pl.BlockSpec((tm, SSD_W), lambda b, s: (tok(b, s), 0)),
            pl.BlockSpec((tm, LANES), lambda b, s: (tok(b, s), 0)),
            pl.BlockSpec((1, LANES), lambda b, s: (0, 0)),
            pl.BlockSpec((1, LANES), lambda b, s: (0, 0)),
            pl.BlockSpec((LANES, SSD_W), lambda b, s: (0, 0)),
            pl.BlockSpec((1, GN, GROUP_W), lambda b, s: (b, 0, 0)),
            pl.BlockSpec((1, ck, GN, GROUP_W), lambda b, s: (b, s, 0, 0)),
            pl.BlockSpec((1, SSD_W), lambda b, s: (0, 0)),
        ],
        out_specs=[
            pl.BlockSpec((tm, SSD_W), lambda b, s: (tok(b, s), 0)),
            pl.BlockSpec((1, GN, GROUP_W), lambda b, s: (b, 0, 0)),
        ],
        out_shape=[
            jax.ShapeDtypeStruct((tokens, SSD_W), BF16),
            jax.ShapeDtypeStruct((n_seq, GN, GROUP_W), F32),
        ],
        scratch_shapes=[pltpu.VMEM((GN, GROUP_W), F32), pltpu.VMEM((tm, SSD_W), F32)],
        compiler_params=pltpu.CompilerParams(
            dimension_semantics=("parallel", "arbitrary"), vmem_limit_bytes=VMEM_LIMIT),
        name="ssd_main",
    )(zx, zx, zx, zr, dt, bias, a_neg, e_mat, h0f, hb_starts, dskip)


def _out_kernel(ya_ref, u_ref, v_ref, zm_ref, x_ref, gate_ref, gv_ref,
                ws_ref, bs_ref, wout_ref, o_ref, sg_ref):
    tm = x_ref.shape[0]
    vn = (_rms(v_ref[...].astype(F32)) * gv_ref[...]).astype(BF16)
    for c in range(tm // MLP_CHUNK):
        rows = slice(c * MLP_CHUNK, (c + 1) * MLP_CHUNK)
        for g in range(MLP_GROUPS):
            cols = slice(g * MLP_GROUP_DIM, (g + 1) * MLP_GROUP_DIM)
            sg_ref[rows, cols] = jnp.dot(ws_ref[g], vn[rows, cols], preferred_element_type=F32) + bs_ref[:, cols]
    yb = _rms(u_ref[...].astype(F32) * sg_ref[...] * _silu(zm_ref[...].astype(F32))).astype(BF16)
    mixed = (jnp.dot(ya_ref[...], wout_ref[0:SSD_W, :], preferred_element_type=F32)
             + jnp.dot(yb, wout_ref[SSD_W:, :], preferred_element_type=F32))
    o_ref[...] = x_ref[...] + gate_ref[0] * _rms(mixed)


def _out(ya, zr, x2, gate, g_v, bs_mat, wts, layer, *, seq_len, tm):
    tokens, d = x2.shape
    assert gate.shape[0] == 1 or seq_len % tm == 0
    mod_row = (lambda i: 0) if gate.shape[0] == 1 else (lambda i: (i * tm) // seq_len)
    col0 = SSD_W // MLP_W
    const2 = lambda i: (0, 0)
    return pl.pallas_call(
        _out_kernel,
        grid=(tokens // tm,),
        in_specs=[
            pl.BlockSpec((tm, SSD_W), lambda i: (i, 0)),
            pl.BlockSpec((tm, MLP_W), lambda i: (i, col0)),
            pl.BlockSpec((tm, MLP_W), lambda i: (i, col0 + 1)),
            pl.BlockSpec((tm, MLP_W), lambda i: (i, col0 + 2)),
            pl.BlockSpec((tm, d), lambda i: (i, 0)),
            pl.BlockSpec((1, 1, d), lambda i: (mod_row(i), 0, 0)),
            pl.BlockSpec((1, MLP_W), const2),
            pl.BlockSpec((None, MLP_GROUPS, MLP_CHUNK, MLP_CHUNK), lambda i: (layer, 0, 0, 0)),
            pl.BlockSpec((MLP_CHUNK, MLP_W), const2),
            pl.BlockSpec((None, SSD_W + MLP_W, d), lambda i: (layer, 0, 0), pipeline_mode=pl.Buffered(1)),
        ],
        out_specs=pl.BlockSpec((tm, d), lambda i: (i, 0)),
        out_shape=jax.ShapeDtypeStruct((tokens, d), F32),
        scratch_shapes=[pltpu.VMEM((tm, MLP_W), F32)],
        compiler_params=pltpu.CompilerParams(
            dimension_semantics=("parallel",), vmem_limit_bytes=VMEM_LIMIT),
        name="mix_out",
    )(ya, zr, zr, zr, x2, gate, g_v, wts["w_s"], bs_mat, wts["w_out"])


def _expand_matrix():
    j = np.arange(LANES)[:, None]
    h = np.arange(SSD_W)[None, :] // HEAD_DIM
    return jnp.asarray((j % HEADS == h) & (j < 3 * HEADS), dtype=BF16)


def _stream_layer(x2, wts, p, layer, mod_rows, h0f, h0b, *, n_seq, seq_len, row_len, tm_in, tm_out):
    d = x2.shape[1]
    shift = mod_rows[:, None, 0:d]
    gs = p["g_pre"] * (1.0 + mod_rows[:, None, d:2 * d])
    gate = mod_rows[:, None, 2 * d:3 * d] * p["g_post"]
    zx, zr, dt = _inproj(x2, gs, shift, wts, layer, seq_len=seq_len, row_len=row_len, tm=tm_in)
    ck = min(SCAN_CHUNKS, seq_len // CHUNK)
    hb_starts, hb_fin = _bwd_scan(zx, dt, p["bias"], p["a_neg"], p["e_mat"], h0b,
                                  n_seq=n_seq, seq_len=seq_len, ck=ck)
    ya, hf_fin = _ssd(zx, zr, dt, p["bias"], p["a_neg"], p["e_mat"], h0f, hb_starts,
                      p["dskip"], n_seq=n_seq, seq_len=seq_len, ck=ck)
    x_new = _out(ya, zr, x2, gate, p["g_v"], p["bs_mat"], wts, layer,
                 seq_len=seq_len, tm=tm_out)
    return x_new, hf_fin, hb_fin


def _all_layer_weights(w_in, conv_w, conv_b, g_ssd, g_mlp, w_s, w_out):
    o = XBC_W + DT_W
    out_gain = jnp.concatenate([g_ssd, g_mlp], axis=1)[:, :, None]
    w16 = lax.optimization_barrier(w_in.astype(BF16))
    return {
        "w_conv": w16,
        "w_plain": w16[:, :, o:],
        "w_dt": jnp.pad(w16[:, :, XBC_W:o], ((0, 0), (0, 0), (0, LANES - DT_W))),
        "cw8": jnp.pad(conv_w, ((0, 0), (0, 8 - CONV_W), (0, 0))),
        "cb": conv_b[:, None, :],
        "w_s": w_s.astype(BF16),
        "w_out": (w_out * out_gain).astype(BF16),
    }


def _layer_params(l, g_pre, g_post, dt_bias, a_log, d_skip, g_v, b_s, e_mat):
    pad = jnp.zeros((LANES - DT_W,), F32)
    return {
        "g_pre": g_pre[l][None, None, :],
        "g_post": g_post[l][None, :],
        "bias": jnp.concatenate([dt_bias[l].reshape(-1), pad])[None, :],
        "a_neg": jnp.concatenate([-jnp.exp(a_log[l].reshape(-1)), pad])[None, :],
        "dskip": jnp.repeat(d_skip[l], HEAD_DIM)[None, :],
        "g_v": g_v[l][None, :],
        "bs_mat": jnp.repeat(b_s[l].T, MLP_GROUP_DIM, axis=1),
        "e_mat": e_mat,
    }


def kernel(x, c, ctx, c_ctx, w_ada, b_ada, g_pre, g_post, w_in, conv_w, conv_b, dt_bias, a_log,
           d_skip, g_ssd, g_v, w_s, b_s, g_mlp, w_out):
    bsz, seq, d = x.shape
    ctx_len = ctx.shape[1]
    depth = w_in.shape[0]
    assert seq % LATENT_TM_IN == 0 and ctx_len % CHUNK == 0 and ctx_len & (ctx_len - 1) == 0
    assert w_in.shape[2] == MAIN_W + DT_W and bsz + 1 <= 8

    cc = jnp.concatenate([c, c_ctx[None, :], jnp.zeros((8 - bsz - 1, d), F32)], axis=0)
    mod = _modulation(cc, w_ada, b_ada)
    e_mat = _expand_matrix()
    wts = _all_layer_weights(w_in, conv_w, conv_b, g_ssd, g_mlp, w_s, w_out)

    xs = x.reshape(bsz * seq, d)
    cs = ctx.reshape(bsz * ctx_len, d)
    zeros_h = jnp.zeros((bsz, GN, GROUP_W), F32)
    for l in range(depth):
        p = _layer_params(l, g_pre, g_post, dt_bias, a_log, d_skip, g_v, b_s, e_mat)
        cs, h_f, h_b = _stream_layer(cs, wts, p, l, mod[l, bsz:bsz + 1], zeros_h, zeros_h, n_seq=bsz,
                                     seq_len=ctx_len, row_len=ctx_len, tm_in=bsz * ctx_len, tm_out=ctx_len)
        xs, _, _ = _stream_layer(xs, wts, p, l, mod[l, :bsz], h_f, h_b, n_seq=bsz, seq_len=seq,
                                 row_len=GRID_W, tm_in=LATENT_TM_IN, tm_out=LATENT_TM_OUT)
    return xs.reshape(bsz, seq, d)
```

```python
import functools

import jax
import jax.numpy as jnp
import numpy as np
from jax import lax
from jax.experimental import pallas as pl
from jax.experimental.pallas import tpu as pltpu

F32 = jnp.float32
BF16 = jnp.bfloat16

GRID_W = 64
HEADS = 32
HEAD_DIM = 64
GROUPS = 8
HEADS_PER_GROUP = HEADS // GROUPS
STATE = 128
CHUNK = 128
CONV_W = 5
CONV_PAD = CONV_W // 2
MLP_GROUPS = 16
MLP_GROUP_DIM = 128
MLP_CHUNK = 128
EPS = 1e-6
LOG2E = 1.4426950408889634

SSD_W = HEADS * HEAD_DIM
GN = GROUPS * STATE
XBC_W = SSD_W + 2 * GN
DT_W = 2 * HEADS
MLP_W = MLP_GROUPS * MLP_GROUP_DIM
GROUP_W = HEADS_PER_GROUP * HEAD_DIM
REST_W = SSD_W + 3 * MLP_W
MAIN_W = XBC_W + REST_W
INPROJ_STEPS = 8
CONV_SLAB = XBC_W // INPROJ_STEPS
PLAIN_SLAB = REST_W // INPROJ_STEPS
LANES = 128
SCAN_CHUNKS = 4
LATENT_TM_IN = 1024
LATENT_TM_OUT = 256

VMEM_LIMIT = 56 * 1024 * 1024


def _silu(v):
    h = 0.5 * v
    return h * (1.0 + jnp.tanh(h))


def _rms(v):
    return v * lax.rsqrt(jnp.mean(v * v, axis=-1, keepdims=True) + EPS)


def _split3(v):
    v1 = v.astype(BF16)
    r1 = v - v1.astype(F32)
    v2 = r1.astype(BF16)
    r2 = r1 - v2.astype(F32)
    return v1, v2, r2.astype(BF16)


def _mod_kernel(cc_ref, w_ref, b_ref, o_ref):
    sc = _silu(cc_ref[...])
    o_ref[0] = jnp.dot(sc, w_ref[0], preferred_element_type=F32) + b_ref[0]


def _modulation(cc, w_ada, b_ada):
    depth, d, d3 = w_ada.shape
    tn = 768
    return pl.pallas_call(
        _mod_kernel,
        grid=(depth, d3 // tn),
        in_specs=[
            pl.BlockSpec((8, d), lambda l, j: (0, 0)),
            pl.BlockSpec((1, d, tn), lambda l, j: (l, 0, j)),
            pl.BlockSpec((1, 1, tn), lambda l, j: (l, 0, j)),
        ],
        out_specs=pl.BlockSpec((1, 8, tn), lambda l, j: (l, 0, j)),
        out_shape=jax.ShapeDtypeStruct((depth, 8, d3), F32),
        compiler_params=pltpu.CompilerParams(
            dimension_semantics=("parallel", "parallel"), vmem_limit_bytes=VMEM_LIMIT),
        name="adaln_mod",
    )(cc, w_ada, b_ada.reshape(depth, 1, d3))


def _inproj_kernel(x_ref, gs_ref, sh_ref, wc_ref, wp_ref, wdt_ref, cw_ref, cb_ref,
                   zx_ref, zr_ref, dt_ref, hx_ref, *, row_len):
    j = pl.program_id(1)
    tm, tc = zx_ref.shape

    @pl.when(j == 0)
    def _():
        hb = (_rms(x_ref[...]) * gs_ref[0] + sh_ref[0]).astype(BF16)
        hx_ref[...] = hb
        dt_ref[...] = jnp.dot(hb, wdt_ref[...], preferred_element_type=F32)

    hx = hx_ref[...]
    acc = jnp.dot(hx, wc_ref[...], preferred_element_type=F32)
    zr_ref[...] = jnp.dot(hx, wp_ref[...], preferred_element_type=F32).astype(BF16)

    rows3 = acc.reshape(tm // row_len, row_len, tc)
    y = cb_ref[...] + cw_ref[CONV_PAD:CONV_PAD + 1, :] * acc
    for d in (-2, -1, 1, 2):
        pad = jnp.zeros((tm // row_len, abs(d), tc), F32)
        if d < 0:
            shifted = jnp.concatenate([pad, rows3[:, :row_len + d, :]], axis=1)
        else:
            shifted = jnp.concatenate([rows3[:, d:, :], pad], axis=1)
        y = y + cw_ref[CONV_PAD + d:CONV_PAD + d + 1, :] * shifted.reshape(tm, tc)
    zx_ref[...] = _silu(y).astype(BF16)


def _inproj(x2, gs, shift, wts, layer, *, seq_len, row_len, tm):
    tokens, d = x2.shape
    assert gs.shape[0] == 1 or seq_len % tm == 0
    mod_row = (lambda i: 0) if gs.shape[0] == 1 else (lambda i: (i * tm) // seq_len)
    kern = functools.partial(_inproj_kernel, row_len=row_len)
    return pl.pallas_call(
        kern,
        grid=(tokens // tm, INPROJ_STEPS),
        in_specs=[
            pl.BlockSpec((tm, d), lambda i, j: (i, 0)),
            pl.BlockSpec((1, 1, d), lambda i, j: (mod_row(i), 0, 0)),
            pl.BlockSpec((1, 1, d), lambda i, j: (mod_row(i), 0, 0)),
            pl.BlockSpec((None, d, CONV_SLAB), lambda i, j: (layer, 0, j)),
            pl.BlockSpec((None, d, PLAIN_SLAB), lambda i, j: (layer, 0, j)),
            pl.BlockSpec((None, d, LANES), lambda i, j: (layer, 0, 0)),
            pl.BlockSpec((None, 8, CONV_SLAB), lambda i, j: (layer, 0, j)),
            pl.BlockSpec((None, 1, CONV_SLAB), lambda i, j: (layer, 0, j)),
        ],
        out_specs=[
            pl.BlockSpec((tm, CONV_SLAB), lambda i, j: (i, j)),
            pl.BlockSpec((tm, PLAIN_SLAB), lambda i, j: (i, j)),
            pl.BlockSpec((tm, LANES), lambda i, j: (i, 0)),
        ],
        out_shape=[
            jax.ShapeDtypeStruct((tokens, XBC_W), BF16),
            jax.ShapeDtypeStruct((tokens, REST_W), BF16),
            jax.ShapeDtypeStruct((tokens, LANES), F32),
        ],
        scratch_shapes=[pltpu.VMEM((tm, d), BF16)],
        compiler_params=pltpu.CompilerParams(
            dimension_semantics=("parallel", "arbitrary"), vmem_limit_bytes=VMEM_LIMIT),
        name="inproj",
    )(x2, gs, shift, wts["w_conv"], wts["w_plain"], wts["w_dt"], wts["cw8"], wts["cb"])


def _chunk_prep(dtr, bias, a_neg):
    q = dtr.shape[0]
    xx = dtr + bias
    dt = jnp.maximum(xx, 0.0) + jnp.log1p(jnp.exp(-jnp.abs(xx)))
    a = dt * a_neg * LOG2E
    acat = jnp.concatenate(_split3(a), axis=0)
    row = lax.broadcasted_iota(jnp.int32, (q, q), 0)
    col = lax.broadcasted_iota(jnp.int32, (q, q), 1)
    t_inc = jnp.where(col <= row, 1.0, 0.0).astype(BF16)
    t_suf = jnp.where(col >= row, 1.0, 0.0).astype(BF16)
    incl = jnp.dot(jnp.concatenate([t_inc] * 3, axis=1), acat, preferred_element_type=F32)
    suf = jnp.dot(jnp.concatenate([t_suf] * 3, axis=1), acat, preferred_element_type=F32)
    lane = lax.broadcasted_iota(jnp.int32, (q, LANES), 1)
    fwd = lane < HEADS
    cum = jnp.where(fwd, incl, suf)
    tot = jnp.where(fwd, incl[q - 1:q, :], suf[0:1, :])
    w = dt * jnp.exp2(tot - cum)
    lo = lane < DT_W
    dt_sum = dt + pltpu.roll(dt, LANES - HEADS, axis=1)
    rowp = jnp.where(lo, cum - jnp.log2(dt), pltpu.roll(jnp.log2(dt_sum), DT_W, axis=1))
    vals = jnp.where(lo, jnp.exp2(cum), pltpu.roll(w, DT_W, axis=1))
    return cum, rowp, vals


def _expand(vals, wanted, e_ref):
    v1 = vals.astype(BF16).astype(F32)
    r1 = vals - v1
    v2 = r1.astype(BF16).astype(F32)
    terms = (v1, v2, r1 - v2)
    packed = []
    for which, rows in wanted:
        n = rows.stop - rows.start
        lane = lax.broadcasted_iota(jnp.int32, (n, LANES), 1)
        p = jnp.zeros((n, LANES), F32)
        for k in (2, 1, 0):
            t = terms[k][rows]
            shift = (HEADS * (k - which)) % LANES
            t = t if shift == 0 else pltpu.roll(t, shift, axis=1)
            p = jnp.where(lane < HEADS * (k + 1), t, p)
        packed.append(p.astype(BF16))
    stacked = packed[0] if len(packed) == 1 else jnp.concatenate(packed, axis=0)
    return jnp.dot(stacked, e_ref[...], preferred_element_type=F32)


def _state_outer(b_g, xw_g):
    return lax.dot_general(b_g, xw_g, (((0,), (0,)), ((), ())), preferred_element_type=F32)


def _bwd_kernel(x_ref, b_ref, dt_ref, bias_ref, a_ref, e_ref, h0_ref,
                hs_ref, hfin_ref, cum_ref, rowp_ref, vals_ref, st_ref, *, ck):
    s = pl.program_id(1)

    @pl.when(s == 0)
    def _():
        st_ref[...] = h0_ref[0]

    q = CHUNK
    for ci in reversed(range(ck)):
        tok = slice(ci * q, (ci + 1) * q)
        cum, rowp, vals = _chunk_prep(dt_ref[tok, :], bias_ref[...], a_ref[...])
        cum_ref[tok, :] = cum
        rowp_ref[tok, :] = rowp
        vals_ref[tok, :] = vals
        ex = _expand(vals, [(3, slice(0, q)), (1, slice(0, 16))], e_ref)
        hs_ref[0, ci] = st_ref[...].astype(BF16)
        for g in range(GROUPS):
            rows = slice(g * STATE, (g + 1) * STATE)
            cols = slice(g * GROUP_W, (g + 1) * GROUP_W)
            xw = (x_ref[tok, cols].astype(F32) * ex[0:q, cols]).astype(BF16)
            dec = ex[q:q + 1, cols]
            st_ref[rows, :] = st_ref[rows, :] * dec + _state_outer(b_ref[tok, rows], xw)

    @pl.when(s == pl.num_programs(1) - 1)
    def _():
        hfin_ref[0] = st_ref[...]


def _bwd_scan(zx, dt, bias, a_neg, e_mat, h0, *, n_seq, seq_len, ck):
    ns = seq_len // (ck * CHUNK)
    nc = seq_len // CHUNK
    tm = ck * CHUNK
    tok = lambda b, s: b * ns + (ns - 1 - s)
    return pl.pallas_call(
        functools.partial(_bwd_kernel, ck=ck),
        grid=(n_seq, ns),
        in_specs=[
            pl.BlockSpec((tm, SSD_W), lambda b, s: (tok(b, s), 0)),
            pl.BlockSpec((tm, GN), lambda b, s: (tok(b, s), SSD_W // GN)),
            pl.BlockSpec((tm, LANES), lambda b, s: (tok(b, s), 0)),
            pl.BlockSpec((1, LANES), lambda b, s: (0, 0)),
            pl.BlockSpec((1, LANES), lambda b, s: (0, 0)),
            pl.BlockSpec((LANES, SSD_W), lambda b, s: (0, 0)),
            pl.BlockSpec((1, GN, GROUP_W), lambda b, s: (b, 0, 0)),
        ],
        out_specs=[
            pl.BlockSpec((1, ck, GN, GROUP_W), lambda b, s: (b, ns - 1 - s, 0, 0)),
            pl.BlockSpec((1, GN, GROUP_W), lambda b, s: (b, 0, 0)),
            pl.BlockSpec((tm, LANES), lambda b, s: (tok(b, s), 0)),
            pl.BlockSpec((tm, LANES), lambda b, s: (tok(b, s), 0)),
            pl.BlockSpec((tm, LANES), lambda b, s: (tok(b, s), 0)),
        ],
        out_shape=[
            jax.ShapeDtypeStruct((n_seq, nc, GN, GROUP_W), BF16),
            jax.ShapeDtypeStruct((n_seq, GN, GROUP_W), F32),
            jax.ShapeDtypeStruct((n_seq * seq_len, LANES), F32),
            jax.ShapeDtypeStruct((n_seq * seq_len, LANES), F32),
            jax.ShapeDtypeStruct((n_seq * seq_len, LANES), F32),
        ],
        scratch_shapes=[pltpu.VMEM((GN, GROUP_W), F32)],
        compiler_params=pltpu.CompilerParams(
            dimension_semantics=("parallel", "arbitrary"), vmem_limit_bytes=VMEM_LIMIT),
        name="ssd_bwd_scan",
    )(zx, zx, dt, bias, a_neg, e_mat, h0)


def _ssd_kernel(x_ref, b_ref, c_ref, zs_ref, cum_ref, rowp_ref, vals_ref, e_ref, h0_ref, hb_ref,
                dskip_ref, ya_ref, hfin_ref, st_ref, y_ref, *, ck):
    s = pl.program_id(1)
    q = CHUNK

    @pl.when(s == 0)
    def _():
        st_ref[...] = h0_ref[0]

    row = lax.broadcasted_iota(jnp.int32, (q, q), 0)
    col = lax.broadcasted_iota(jnp.int32, (q, q), 1)
    lower = col < row
    upper = col > row
    first_head = lax.broadcasted_iota(jnp.int32, (q, 2 * HEAD_DIM), 1) < HEAD_DIM

    for ci in range(ck):
        tok = slice(ci * q, (ci + 1) * q)
        cum, vals = cum_ref[tok, :], vals_ref[tok, :]
        row_t = rowp_ref[tok, :].T
        ex = _expand(vals, [(0, slice(0, q)), (1, slice(0, q)), (2, slice(0, q))], e_ref)

        for g in range(GROUPS):
            rows = slice(g * STATE, (g + 1) * STATE)
            cols = slice(g * GROUP_W, (g + 1) * GROUP_W)
            x_g = x_ref[tok, cols]
            xf_g = x_g.astype(F32)
            b_g = b_ref[tok, rows]
            c_g = c_ref[tok, rows]
            cb = lax.dot_general(c_g, b_g, (((1,), (1,)), ((), ())), preferred_element_type=F32)
            ms = []
            for r in range(HEADS_PER_GROUP):
                h = g * HEADS_PER_GROUP + r
                cum_f = jnp.broadcast_to(cum[:, h:h + 1], (q, q))
                cum_b = jnp.broadcast_to(cum[:, HEADS + h:HEADS + h + 1], (q, q))
                arg = jnp.where(lower, cum_f - row_t[h:h + 1, :],
                                jnp.where(upper, cum_b - row_t[HEADS + h:HEADS + h + 1, :],
                                          row_t[2 * HEADS + h:2 * HEADS + h + 1, :]))
                ms.append((cb * jnp.exp2(arg)).astype(BF16))
            ys = []
            for pr in range(HEADS_PER_GROUP // 2):
                xp = x_g[:, pr * 2 * HEAD_DIM:(pr + 1) * 2 * HEAD_DIM]
                zero = jnp.zeros_like(xp)
                rhs = jnp.concatenate([jnp.where(first_head, xp, zero), jnp.where(first_head, zero, xp)], axis=0)
                lhs = jnp.concatenate([ms[2 * pr], ms[2 * pr + 1]], axis=1)
                ys.append(jnp.dot(lhs, rhs, preferred_element_type=F32))
            y_diag = jnp.concatenate(ys, axis=1)
            e_f = ex[0:q, cols]
            y_off = (jnp.dot(c_g, st_ref[rows, :].astype(BF16), preferred_element_type=F32) * e_f
                     + jnp.dot(c_g, hb_ref[0, ci, rows, :], preferred_element_type=F32) * ex[q:2 * q, cols])
            y_ref[tok, cols] = y_diag + y_off + dskip_ref[:, cols] * xf_g
            xw = (xf_g * ex[2 * q:3 * q, cols]).astype(BF16)
            st_ref[rows, :] = st_ref[rows, :] * e_f[q - 1:q, :] + _state_outer(b_g, xw)

        gated = y_ref[tok, :] * _silu(zs_ref[tok, :].astype(F32))
        ya_ref[tok, :] = _rms(gated).astype(BF16)

    @pl.when(s == pl.num_programs(1) - 1)
    def _():
        hfin_ref[0] = st_ref[...]


def _ssd(zx, zr, cum, rowp, vals, e_mat, h0f, hb_starts, dskip, *, n_seq, seq_len, ck):
    ns = seq_len // (ck * CHUNK)
    tm = ck * CHUNK
    tok = lambda b, s: b * ns + s
    tokens = n_seq * seq_len
    return pl.pallas_call(
        functools.partial(_ssd_kernel, ck=ck),
        grid=(n_seq, ns),
        in_specs=[
            pl.BlockSpec((tm, SSD_W), lambda b, s: (tok(b, s), 0)),
            pl.BlockSpec((tm, GN), lambda b, s: (tok(b, s), SSD_W // GN)),
            pl.BlockSpec((tm, GN), lambda b, s: (tok(b, s), SSD_W // GN + 1)),
            pl.BlockSpec((tm, SSD_W), lambda b, s: (tok(b, s), 0)),
            pl.BlockSpec((tm, LANES), lambda b, s: (tok(b, s), 0)),
            pl.BlockSpec((tm, LANES), lambda b, s: (tok(b, s), 0)),
            pl.BlockSpec((tm, LANES), lambda b, s: (tok(b, s), 0)),
            pl.BlockSpec((LANES, SSD_W), lambda b, s: (0, 0)),
            pl.BlockSpec((1, GN, GROUP_W), lambda b, s: (b, 0, 0)),
            pl.BlockSpec((1, ck, GN, GROUP_W), lambda b, s: (b, s, 0, 0)),
            pl.BlockSpec((1, SSD_W), lambda b, s: (0, 0)),
        ],
        out_specs=[
            pl.BlockSpec((tm, SSD_W), lambda b, s: (tok(b, s), 0)),
            pl.BlockSpec((1, GN, GROUP_W), lambda b, s: (b, 0, 0)),
        ],
        out_shape=[
            jax.ShapeDtypeStruct((tokens, SSD_W), BF16),
            jax.ShapeDtypeStruct((n_seq, GN, GROUP_W), F32),
        ],
        scratch_shapes=[pltpu.VMEM((GN, GROUP_W), F32), pltpu.VMEM((tm, SSD_W), F32)],
        compiler_params=pltpu.CompilerParams(
            dimension_semantics=("parallel", "arbitrary"), vmem_limit_bytes=VMEM_LIMIT),
        name="ssd_main",
    )(zx, zx, zx, zr, cum, rowp, vals, e_mat, h0f, hb_starts, dskip)


def _out_kernel(ya_ref, u_ref, v_ref, zm_ref, x_ref, gate_ref, gv_ref,
                ws_ref, bs_ref, wout_ref, o_ref, sg_ref):
    tm = x_ref.shape[0]
    vn = (_rms(v_ref[...].astype(F32)) * gv_ref[...]).astype(BF16)
    for c in range(tm // MLP_CHUNK):
        rows = slice(c * MLP_CHUNK, (c + 1) * MLP_CHUNK)
        for g in range(MLP_GROUPS):
            cols = slice(g * MLP_GROUP_DIM, (g + 1) * MLP_GROUP_DIM)
            sg_ref[rows, cols] = jnp.dot(ws_ref[g], vn[rows, cols], preferred_element_type=F32) + bs_ref[:, cols]
    yb = _rms(u_ref[...].astype(F32) * sg_ref[...] * _silu(zm_ref[...].astype(F32))).astype(BF16)
    mixed = (jnp.dot(ya_ref[...], wout_ref[0:SSD_W, :], preferred_element_type=F32)
             + jnp.dot(yb, wout_ref[SSD_W:, :], preferred_element_type=F32))
    o_ref[...] = x_ref[...] + gate_ref[0] * _rms(mixed)


def _out(ya, zr, x2, gate, g_v, bs_mat, wts, layer, *, seq_len, tm):
    tokens, d = x2.shape
    assert gate.shape[0] == 1 or seq_len % tm == 0
    mod_row = (lambda i: 0) if gate.shape[0] == 1 else (lambda i: (i * tm) // seq_len)
    col0 = SSD_W // MLP_W
    const2 = lambda i: (0, 0)
    return pl.pallas_call(
        _out_kernel,
        grid=(tokens // tm,),
        in_specs=[
            pl.BlockSpec((tm, SSD_W), lambda i: (i, 0)),
            pl.BlockSpec((tm, MLP_W), lambda i: (i, col0)),
            pl.BlockSpec((tm, MLP_W), lambda i: (i, col0 + 1)),
            pl.BlockSpec((tm, MLP_W), lambda i: (i, col0 + 2)),
            pl.BlockSpec((tm, d), lambda i: (i, 0)),
            pl.BlockSpec((1, 1, d), lambda i: (mod_row(i), 0, 0)),
            pl.BlockSpec((1, MLP_W), const2),
            pl.BlockSpec((None, MLP_GROUPS, MLP_CHUNK, MLP_CHUNK), lambda i: (layer, 0, 0, 0)),
            pl.BlockSpec((MLP_CHUNK, MLP_W), const2),
            pl.BlockSpec((None, SSD_W + MLP_W, d), lambda i: (layer, 0, 0), pipeline_mode=pl.Buffered(1)),
        ],
        out_specs=pl.BlockSpec((tm, d), lambda i: (i, 0)),
        out_shape=jax.ShapeDtypeStruct((tokens, d), F32),
        scratch_shapes=[pltpu.VMEM((tm, MLP_W), F32)],
        compiler_params=pltpu.CompilerParams(
            dimension_semantics=("parallel",), vmem_limit_bytes=VMEM_LIMIT),
        name="mix_out",
    )(ya, zr, zr, zr, x2, gate, g_v, wts["w_s"], bs_mat, wts["w_out"])


def _expand_matrix():
    j = np.arange(LANES)[:, None]
    h = np.arange(SSD_W)[None, :] // HEAD_DIM
    return jnp.asarray((j % HEADS == h) & (j < 3 * HEADS), dtype=BF16)


def _stream_layer(x2, wts, p, layer, mod_rows, h0f, h0b, *, n_seq, seq_len, row_len, tm_in, tm_out):
    d = x2.shape[1]
    shift = mod_rows[:, None, 0:d]
    gs = p["g_pre"] * (1.0 + mod_rows[:, None, d:2 * d])
    gate = mod_rows[:, None, 2 * d:3 * d] * p["g_post"]
    zx, zr, dt = _inproj(x2, gs, shift, wts, layer, seq_len=seq_len, row_len=row_len, tm=tm_in)
    ck = min(SCAN_CHUNKS, seq_len // CHUNK)
    hb_starts, hb_fin, cum, rowp, vals = _bwd_scan(zx, dt, p["bias"], p["a_neg"], p["e_mat"], h0b,
                                                   n_seq=n_seq, seq_len=seq_len, ck=ck)
    ya, hf_fin = _ssd(zx, zr, cum, rowp, vals, p["e_mat"], h0f, hb_starts,
                      p["dskip"], n_seq=n_seq, seq_len=seq_len, ck=ck)
    x_new = _out(ya, zr, x2, gate, p["g_v"], p["bs_mat"], wts, layer,
                 seq_len=seq_len, tm=tm_out)
    return x_new, hf_fin, hb_fin


def _all_layer_weights(w_in, conv_w, conv_b, g_ssd, g_mlp, w_s, w_out):
    o = XBC_W + DT_W
    out_gain = jnp.concatenate([g_ssd, g_mlp], axis=1)[:, :, None]
    w16 = lax.optimization_barrier(w_in.astype(BF16))
    return {
        "w_conv": w16[:, :, :XBC_W],
        "w_plain": w16[:, :, o:],
        "w_dt": jnp.pad(w16[:, :, XBC_W:o], ((0, 0), (0, 0), (0, LANES - DT_W))),
        "cw8": jnp.pad(conv_w, ((0, 0), (0, 8 - CONV_W), (0, 0))),
        "cb": conv_b[:, None, :],
        "w_s": w_s.astype(BF16),
        "w_out": (w_out * out_gain).astype(BF16),
    }


def _layer_params(l, g_pre, g_post, dt_bias, a_log, d_skip, g_v, b_s, e_mat):
    pad = jnp.zeros((LANES - DT_W,), F32)
    return {
        "g_pre": g_pre[l][None, None, :],
        "g_post": g_post[l][None, :],
        "bias": jnp.concatenate([dt_bias[l].reshape(-1), pad])[None, :],
        "a_neg": jnp.concatenate([-jnp.exp(a_log[l].reshape(-1)), pad])[None, :],
        "dskip": jnp.repeat(d_skip[l], HEAD_DIM)[None, :],
        "g_v": g_v[l][None, :],
        "bs_mat": jnp.repeat(b_s[l].T, MLP_GROUP_DIM, axis=1),
        "e_mat": e_mat,
    }


def kernel(x, c, ctx, c_ctx, w_ada, b_ada, g_pre, g_post, w_in, conv_w, conv_b, dt_bias, a_log,
           d_skip, g_ssd, g_v, w_s, b_s, g_mlp, w_out):
    bsz, seq, d = x.shape
    ctx_len = ctx.shape[1]
    depth = w_in.shape[0]
    assert seq % LATENT_TM_IN == 0 and ctx_len % CHUNK == 0 and ctx_len & (ctx_len - 1) == 0
    assert w_in.shape[2] == MAIN_W + DT_W and bsz + 1 <= 8

    cc = jnp.concatenate([c, c_ctx[None, :], jnp.zeros((8 - bsz - 1, d), F32)], axis=0)
    mod = _modulation(cc, w_ada, b_ada)
    e_mat = _expand_matrix()
    wts = _all_layer_weights(w_in, conv_w, conv_b, g_ssd, g_mlp, w_s, w_out)

    xs = x.reshape(bsz * seq, d)
    cs = ctx.reshape(bsz * ctx_len, d)
    zeros_h = jnp.zeros((bsz, GN, GROUP_W), F32)
    for l in range(depth):
        p = _layer_params(l, g_pre, g_post, dt_bias, a_log, d_skip, g_v, b_s, e_mat)
        cs, h_f, h_b = _stream_layer(cs, wts, p, l, mod[l, bsz:bsz + 1], zeros_h, zeros_h, n_seq=bsz,
                                     seq_len=ctx_len, row_len=ctx_len, tm_in=bsz * ctx_len, tm_out=ctx_len)
        xs, _, _ = _stream_layer(xs, wts, p, l, mod[l, :bsz], h_f, h_b, n_seq=bsz, seq_len=seq,
                                 row_len=GRID_W, tm_in=LATENT_TM_IN, tm_out=LATENT_TM_OUT)
    return xs.reshape(bsz, seq, d)
```

```python
import functools

import jax
import jax.numpy as jnp
import numpy as np
from jax import lax
from jax.experimental import pallas as pl
from jax.experimental.pallas import tpu as pltpu

F32 = jnp.float32
BF16 = jnp.bfloat16

GRID_W = 64
HEADS = 32
HEAD_DIM = 64
GROUPS = 8
HEADS_PER_GROUP = HEADS // GROUPS
STATE = 128
CHUNK = 128
CONV_W = 5
CONV_PAD = CONV_W // 2
MLP_GROUPS = 16
MLP_GROUP_DIM = 128
MLP_CHUNK = 128
EPS = 1e-6
LOG2E = 1.4426950408889634

SSD_W = HEADS * HEAD_DIM
GN = GROUPS * STATE
XBC_W = SSD_W + 2 * GN
DT_W = 2 * HEADS
MLP_W = MLP_GROUPS * MLP_GROUP_DIM
GROUP_W = HEADS_PER_GROUP * HEAD_DIM
REST_W = SSD_W + 3 * MLP_W
MAIN_W = XBC_W + REST_W
INPROJ_STEPS = 8
CONV_SLAB = XBC_W // INPROJ_STEPS
PLAIN_SLAB = REST_W // INPROJ_STEPS
LANES = 128
SCAN_CHUNKS = 4
LATENT_TM_IN = 1024
LATENT_TM_OUT = 256

VMEM_LIMIT = 56 * 1024 * 1024


def _silu(v):
    h = 0.5 * v
    return h * (1.0 + jnp.tanh(h))


def _rms(v):
    return v * lax.rsqrt(jnp.mean(v * v, axis=-1, keepdims=True) + EPS)


def _dot_nt(a, b):
    return lax.dot_general(a, b, (((1,), (1,)), ((), ())), preferred_element_type=F32)


def _split3(v):
    v1 = v.astype(BF16)
    r1 = v - v1.astype(F32)
    v2 = r1.astype(BF16)
    r2 = r1 - v2.astype(F32)
    return v1, v2, r2.astype(BF16)


def _mod_kernel(cc_ref, w_ref, b_ref, o_ref):
    sc = _silu(cc_ref[...])
    o_ref[0] = jnp.dot(sc, w_ref[0], preferred_element_type=F32) + b_ref[0]


def _modulation(cc, w_ada, b_ada):
    depth, d, d3 = w_ada.shape
    tn = 768
    return pl.pallas_call(
        _mod_kernel,
        grid=(depth, d3 // tn),
        in_specs=[
            pl.BlockSpec((8, d), lambda l, j: (0, 0)),
            pl.BlockSpec((1, d, tn), lambda l, j: (l, 0, j)),
            pl.BlockSpec((1, 1, tn), lambda l, j: (l, 0, j)),
        ],
        out_specs=pl.BlockSpec((1, 8, tn), lambda l, j: (l, 0, j)),
        out_shape=jax.ShapeDtypeStruct((depth, 8, d3), F32),
        compiler_params=pltpu.CompilerParams(
            dimension_semantics=("parallel", "parallel"), vmem_limit_bytes=VMEM_LIMIT),
        name="adaln_mod",
    )(cc, w_ada, b_ada.reshape(depth, 1, d3))


def _inproj_kernel(x_ref, gs_ref, sh_ref, wc_ref, wp_ref, wdt_ref, cw_ref, cb_ref,
                   zx_ref, zr_ref, dt_ref, hx_ref, *, row_len):
    j = pl.program_id(1)
    tm, tc = zx_ref.shape

    @pl.when(j == 0)
    def _():
        hb = (_rms(x_ref[...]) * gs_ref[0] + sh_ref[0]).astype(BF16)
        hx_ref[...] = hb
        dt_ref[...] = _dot_nt(hb, wdt_ref[...])

    hx = hx_ref[...]
    acc = _dot_nt(hx, wc_ref[...])
    zr_ref[...] = _dot_nt(hx, wp_ref[...]).astype(BF16)

    rows3 = acc.reshape(tm // row_len, row_len, tc)
    y = cb_ref[...] + cw_ref[CONV_PAD:CONV_PAD + 1, :] * acc
    for d in (-2, -1, 1, 2):
        pad = jnp.zeros((tm // row_len, abs(d), tc), F32)
        if d < 0:
            shifted = jnp.concatenate([pad, rows3[:, :row_len + d, :]], axis=1)
        else:
            shifted = jnp.concatenate([rows3[:, d:, :], pad], axis=1)
        y = y + cw_ref[CONV_PAD + d:CONV_PAD + d + 1, :] * shifted.reshape(tm, tc)
    zx_ref[...] = _silu(y).astype(BF16)


def _inproj(x2, gs, shift, wts, layer, *, seq_len, row_len, tm):
    tokens, d = x2.shape
    assert gs.shape[0] == 1 or seq_len % tm == 0
    mod_row = (lambda i: 0) if gs.shape[0] == 1 else (lambda i: (i * tm) // seq_len)
    kern = functools.partial(_inproj_kernel, row_len=row_len)
    return pl.pallas_call(
        kern,
        grid=(tokens // tm, INPROJ_STEPS),
        in_specs=[
            pl.BlockSpec((tm, d), lambda i, j: (i, 0)),
            pl.BlockSpec((1, 1, d), lambda i, j: (mod_row(i), 0, 0)),
            pl.BlockSpec((1, 1, d), lambda i, j: (mod_row(i), 0, 0)),
            pl.BlockSpec((None, CONV_SLAB, d), lambda i, j: (layer, j, 0)),
            pl.BlockSpec((pl.Element(PLAIN_SLAB), pl.Element(d)),
                         lambda i, j: (pl.multiple_of(
                             layer * (MAIN_W + DT_W) + XBC_W + DT_W + j * PLAIN_SLAB, DT_W), 0)),
            pl.BlockSpec((None, LANES, d), lambda i, j: (layer, XBC_W // LANES, 0)),
            pl.BlockSpec((None, 8, CONV_SLAB), lambda i, j: (layer, 0, j)),
            pl.BlockSpec((None, 1, CONV_SLAB), lambda i, j: (layer, 0, j)),
        ],
        out_specs=[
            pl.BlockSpec((tm, CONV_SLAB), lambda i, j: (i, j)),
            pl.BlockSpec((tm, PLAIN_SLAB), lambda i, j: (i, j)),
            pl.BlockSpec((tm, LANES), lambda i, j: (i, 0)),
        ],
        out_shape=[
            jax.ShapeDtypeStruct((tokens, XBC_W), BF16),
            jax.ShapeDtypeStruct((tokens, REST_W), BF16),
            jax.ShapeDtypeStruct((tokens, LANES), F32),
        ],
        scratch_shapes=[pltpu.VMEM((tm, d), BF16)],
        compiler_params=pltpu.CompilerParams(
            dimension_semantics=("parallel", "arbitrary"), vmem_limit_bytes=VMEM_LIMIT),
        name="inproj",
    )(x2, gs, shift, wts["w_conv"], wts["w_plain"], wts["w_dt"], wts["cw8"], wts["cb"])


def _chunk_prep(dtr, bias, a_neg):
    q = dtr.shape[0]
    xx = dtr + bias
    dt = jnp.maximum(xx, 0.0) + jnp.log1p(jnp.exp(-jnp.abs(xx)))
    a = dt * a_neg * LOG2E
    acat = jnp.concatenate(_split3(a), axis=0)
    row = lax.broadcasted_iota(jnp.int32, (q, q), 0)
    col = lax.broadcasted_iota(jnp.int32, (q, q), 1)
    t_inc = jnp.where(col <= row, 1.0, 0.0).astype(BF16)
    t_suf = jnp.where(col >= row, 1.0, 0.0).astype(BF16)
    incl = jnp.dot(jnp.concatenate([t_inc] * 3, axis=1), acat, preferred_element_type=F32)
    suf = jnp.dot(jnp.concatenate([t_suf] * 3, axis=1), acat, preferred_element_type=F32)
    lane = lax.broadcasted_iota(jnp.int32, (q, LANES), 1)
    fwd = lane < HEADS
    cum = jnp.where(fwd, incl, suf)
    tot = jnp.where(fwd, incl[q - 1:q, :], suf[0:1, :])
    w = dt * jnp.exp2(tot - cum)
    lo = lane < DT_W
    dt_sum = dt + pltpu.roll(dt, LANES - HEADS, axis=1)
    rowp = jnp.where(lo, cum - jnp.log2(dt), pltpu.roll(jnp.log2(dt_sum), DT_W, axis=1))
    vals = jnp.where(lo, jnp.exp2(cum), pltpu.roll(w, DT_W, axis=1))
    return cum, rowp, vals


def _expand(vals, wanted, e_ref):
    v1 = vals.astype(BF16).astype(F32)
    r1 = vals - v1
    v2 = r1.astype(BF16).astype(F32)
    terms = (v1, v2, r1 - v2)
    packed = []
    for which, rows in wanted:
        n = rows.stop - rows.start
        lane = lax.broadcasted_iota(jnp.int32, (n, LANES), 1)
        p = jnp.zeros((n, LANES), F32)
        for k in (2, 1, 0):
            t = terms[k][rows]
            shift = (HEADS * (k - which)) % LANES
            t = t if shift == 0 else pltpu.roll(t, shift, axis=1)
            p = jnp.where(lane < HEADS * (k + 1), t, p)
        packed.append(p.astype(BF16))
    stacked = packed[0] if len(packed) == 1 else jnp.concatenate(packed, axis=0)
    return jnp.dot(stacked, e_ref[...], preferred_element_type=F32)


def _state_outer(b_g, xw_g):
    return lax.dot_general(b_g, xw_g, (((0,), (0,)), ((), ())), preferred_element_type=F32)


def _bwd_kernel(x_ref, b_ref, dt_ref, bias_ref, a_ref, e_ref, h0_ref,
                hs_ref, hfin_ref, cum_ref, rowp_ref, vals_ref, st_ref, *, ck):
    s = pl.program_id(1)

    @pl.when(s == 0)
    def _():
        st_ref[...] = h0_ref[0]

    q = CHUNK
    for ci in reversed(range(ck)):
        tok = slice(ci * q, (ci + 1) * q)
        cum, rowp, vals = _chunk_prep(dt_ref[tok, :], bias_ref[...], a_ref[...])
        cum_ref[tok, :] = cum
        rowp_ref[tok, :] = rowp
        vals_ref[tok, :] = vals
        ex = _expand(vals, [(3, slice(0, q)), (1, slice(0, 16))], e_ref)
        hs_ref[0, ci] = st_ref[...].astype(BF16)
        for g in range(GROUPS):
            rows = slice(g * STATE, (g + 1) * STATE)
            cols = slice(g * GROUP_W, (g + 1) * GROUP_W)
            xw = (x_ref[tok, cols].astype(F32) * ex[0:q, cols]).astype(BF16)
            dec = ex[q:q + 1, cols]
            st_ref[rows, :] = st_ref[rows, :] * dec + _state_outer(b_ref[tok, rows], xw)

    @pl.when(s == pl.num_programs(1) - 1)
    def _():
        hfin_ref[0] = st_ref[...]


def _bwd_scan(zx, dt, bias, a_neg, e_mat, h0, *, n_seq, seq_len, ck):
    ns = seq_len // (ck * CHUNK)
    nc = seq_len // CHUNK
    tm = ck * CHUNK
    tok = lambda b, s: b * ns + (ns - 1 - s)
    return pl.pallas_call(
        functools.partial(_bwd_kernel, ck=ck),
        grid=(n_seq, ns),
        in_specs=[
            pl.BlockSpec((tm, SSD_W), lambda b, s: (tok(b, s), 0)),
            pl.BlockSpec((tm, GN), lambda b, s: (tok(b, s), SSD_W // GN)),
            pl.BlockSpec((tm, LANES), lambda b, s: (tok(b, s), 0)),
            pl.BlockSpec((1, LANES), lambda b, s: (0, 0)),
            pl.BlockSpec((1, LANES), lambda b, s: (0, 0)),
            pl.BlockSpec((LANES, SSD_W), lambda b, s: (0, 0)),
            pl.BlockSpec((1, GN, GROUP_W), lambda b, s: (b, 0, 0)),
        ],
        out_specs=[
            pl.BlockSpec((1, ck, GN, GROUP_W), lambda b, s: (b, ns - 1 - s, 0, 0)),
            pl.BlockSpec((1, GN, GROUP_W), lambda b, s: (b, 0, 0)),
            pl.BlockSpec((tm, LANES), lambda b, s: (tok(b, s), 0)),
            pl.BlockSpec((tm, LANES), lambda b, s: (tok(b, s), 0)),
            pl.BlockSpec((tm, LANES), lambda b, s: (tok(b, s), 0)),
        ],
        out_shape=[
            jax.ShapeDtypeStruct((n_seq, nc, GN, GROUP_W), BF16),
            jax.ShapeDtypeStruct((n_seq, GN, GROUP_W), F32),
            jax.ShapeDtypeStruct((n_seq * seq_len, LANES), F32),
            jax.ShapeDtypeStruct((n_seq * seq_len, LANES), F32),
            jax.ShapeDtypeStruct((n_seq * seq_len, LANES), F32),
        ],
        scratch_shapes=[pltpu.VMEM((GN, GROUP_W), F32)],
        compiler_params=pltpu.CompilerParams(
            dimension_semantics=("parallel", "arbitrary"), vmem_limit_bytes=VMEM_LIMIT),
        name="ssd_bwd_scan",
    )(zx, zx, dt, bias, a_neg, e_mat, h0)


def _ssd_kernel(x_ref, b_ref, c_ref, zs_ref, cum_ref, rowp_ref, vals_ref, e_ref, h0_ref, hb_ref,
                dskip_ref, ya_ref, hfin_ref, st_ref, y_ref, *, ck):
    s = pl.program_id(1)
    q = CHUNK

    @pl.when(s == 0)
    def _():
        st_ref[...] = h0_ref[0]

    row = lax.broadcasted_iota(jnp.int32, (q, q), 0)
    col = lax.broadcasted_iota(jnp.int32, (q, q), 1)
    lower = col < row
    upper = col > row
    first_head = lax.broadcasted_iota(jnp.int32, (q, 2 * HEAD_DIM), 1) < HEAD_DIM

    for ci in range(ck):
        tok = slice(ci * q, (ci + 1) * q)
        cum, vals = cum_ref[tok, :], vals_ref[tok, :]
        row_t = rowp_ref[tok, :].T
        ex = _expand(vals, [(0, slice(0, q)), (1, slice(0, q)), (2, slice(0, q))], e_ref)

        for g in range(GROUPS):
            rows = slice(g * STATE, (g + 1) * STATE)
            cols = slice(g * GROUP_W, (g + 1) * GROUP_W)
            x_g = x_ref[tok, cols]
            xf_g = x_g.astype(F32)
            b_g = b_ref[tok, rows]
            c_g = c_ref[tok, rows]
            cb = lax.dot_general(c_g, b_g, (((1,), (1,)), ((), ())), preferred_element_type=F32)
            ms = []
            for r in range(HEADS_PER_GROUP):
                h = g * HEADS_PER_GROUP + r
                cum_f = jnp.broadcast_to(cum[:, h:h + 1], (q, q))
                cum_b = jnp.broadcast_to(cum[:, HEADS + h:HEADS + h + 1], (q, q))
                arg = jnp.where(lower, cum_f - row_t[h:h + 1, :],
                                jnp.where(upper, cum_b - row_t[HEADS + h:HEADS + h + 1, :],
                                          row_t[2 * HEADS + h:2 * HEADS + h + 1, :]))
                ms.append((cb * jnp.exp2(arg)).astype(BF16))
            ys = []
            for pr in range(HEADS_PER_GROUP // 2):
                xp = x_g[:, pr * 2 * HEAD_DIM:(pr + 1) * 2 * HEAD_DIM]
                zero = jnp.zeros_like(xp)
                rhs = jnp.concatenate([jnp.where(first_head, xp, zero), jnp.where(first_head, zero, xp)], axis=0)
                lhs = jnp.concatenate([ms[2 * pr], ms[2 * pr + 1]], axis=1)
                ys.append(jnp.dot(lhs, rhs, preferred_element_type=F32))
            y_diag = jnp.concatenate(ys, axis=1)
            e_f = ex[0:q, cols]
            y_off = (jnp.dot(c_g, st_ref[rows, :].astype(BF16), preferred_element_type=F32) * e_f
                     + jnp.dot(c_g, hb_ref[0, ci, rows, :], preferred_element_type=F32) * ex[q:2 * q, cols])
            y_ref[tok, cols] = y_diag + y_off + dskip_ref[:, cols] * xf_g
            xw = (xf_g * ex[2 * q:3 * q, cols]).astype(BF16)
            st_ref[rows, :] = st_ref[rows, :] * e_f[q - 1:q, :] + _state_outer(b_g, xw)

        gated = y_ref[tok, :] * _silu(zs_ref[tok, :].astype(F32))
        ya_ref[tok, :] = _rms(gated).astype(BF16)

    @pl.when(s == pl.num_programs(1) - 1)
    def _():
        hfin_ref[0] = st_ref[...]


def _ssd(zx, zr, cum, rowp, vals, e_mat, h0f, hb_starts, dskip, *, n_seq, seq_len, ck):
    ns = seq_len // (ck * CHUNK)
    tm = ck * CHUNK
    tok = lambda b, s: b * ns + s
    tokens = n_seq * seq_len
    return pl.pallas_call(
        functools.partial(_ssd_kernel, ck=ck),
        grid=(n_seq, ns),
        in_specs=[
            pl.BlockSpec((tm, SSD_W), lambda b, s: (tok(b, s), 0)),
            pl.BlockSpec((tm, GN), lambda b, s: (tok(b, s), SSD_W // GN)),
            pl.BlockSpec((tm, GN), lambda b, s: (tok(b, s), SSD_W // GN + 1)),
            pl.BlockSpec((tm, SSD_W), lambda b, s: (tok(b, s), 0)),
            pl.BlockSpec((tm, LANES), lambda b, s: (tok(b, s), 0)),
            pl.BlockSpec((tm, LANES), lambda b, s: (tok(b, s), 0)),
            pl.BlockSpec((tm, LANES), lambda b, s: (tok(b, s), 0)),
            pl.BlockSpec((LANES, SSD_W), lambda b, s: (0, 0)),
            pl.BlockSpec((1, GN, GROUP_W), lambda b, s: (b, 0, 0)),
            pl.BlockSpec((1, ck, GN, GROUP_W), lambda b, s: (b, s, 0, 0)),
            pl.BlockSpec((1, SSD_W), lambda b, s: (0, 0)),
        ],
        out_specs=[
            pl.BlockSpec((tm, SSD_W), lambda b, s: (tok(b, s), 0)),
            pl.BlockSpec((1, GN, GROUP_W), lambda b, s: (b, 0, 0)),
        ],
        out_shape=[
            jax.ShapeDtypeStruct((tokens, SSD_W), BF16),
            jax.ShapeDtypeStruct((n_seq, GN, GROUP_W), F32),
        ],
        scratch_shapes=[pltpu.VMEM((GN, GROUP_W), F32), pltpu.VMEM((tm, SSD_W), F32)],
        compiler_params=pltpu.CompilerParams(
            dimension_semantics=("parallel", "arbitrary"), vmem_limit_bytes=VMEM_LIMIT),
        name="ssd_main",
    )(zx, zx, zx, zr, cum, rowp, vals, e_mat, h0f, hb_starts, dskip)


def _out_kernel(ya_ref, u_ref, v_ref, zm_ref, x_ref, gate_ref, gv_ref,
                ws_ref, bs_ref, wout_ref, o_ref, sg_ref):
    tm = x_ref.shape[0]
    vn = (_rms(v_ref[...].astype(F32)) * gv_ref[...]).astype(BF16)
    for c in range(tm // MLP_CHUNK):
        rows = slice(c * MLP_CHUNK, (c + 1) * MLP_CHUNK)
        for g in range(MLP_GROUPS):
            cols = slice(g * MLP_GROUP_DIM, (g + 1) * MLP_GROUP_DIM)
            sg_ref[rows, cols] = jnp.dot(ws_ref[g], vn[rows, cols], preferred_element_type=F32) + bs_ref[:, cols]
    yb = _rms(u_ref[...].astype(F32) * sg_ref[...] * _silu(zm_ref[...].astype(F32))).astype(BF16)
    mixed = (jnp.dot(ya_ref[...], wout_ref[0:SSD_W, :], preferred_element_type=F32)
             + jnp.dot(yb, wout_ref[SSD_W:, :], preferred_element_type=F32))
    o_ref[...] = x_ref[...] + gate_ref[0] * _rms(mixed)


def _out(ya, zr, x2, gate, g_v, bs_mat, wts, layer, *, seq_len, tm):
    tokens, d = x2.shape
    assert gate.shape[0] == 1 or seq_len % tm == 0
    mod_row = (lambda i: 0) if gate.shape[0] == 1 else (lambda i: (i * tm) // seq_len)
    col0 = SSD_W // MLP_W
    const2 = lambda i: (0, 0)
    return pl.pallas_call(
        _out_kernel,
        grid=(tokens // tm,),
        in_specs=[
            pl.BlockSpec((tm, SSD_W), lambda i: (i, 0)),
            pl.BlockSpec((tm, MLP_W), lambda i: (i, col0)),
            pl.BlockSpec((tm, MLP_W), lambda i: (i, col0 + 1)),
            pl.BlockSpec((tm, MLP_W), lambda i: (i, col0 + 2)),
            pl.BlockSpec((tm, d), lambda i: (i, 0)),
            pl.BlockSpec((1, 1, d), lambda i: (mod_row(i), 0, 0)),
            pl.BlockSpec((1, MLP_W), const2),
            pl.BlockSpec((None, MLP_GROUPS, MLP_CHUNK, MLP_CHUNK), lambda i: (layer, 0, 0, 0)),
            pl.BlockSpec((MLP_CHUNK, MLP_W), const2),
            pl.BlockSpec((None, SSD_W + MLP_W, d), lambda i: (layer, 0, 0), pipeline_mode=pl.Buffered(1)),
        ],
        out_specs=pl.BlockSpec((tm, d), lambda i: (i, 0)),
        out_shape=jax.ShapeDtypeStruct((tokens, d), F32),
        scratch_shapes=[pltpu.VMEM((tm, MLP_W), F32)],
        compiler_params=pltpu.CompilerParams(
            dimension_semantics=("parallel",), vmem_limit_bytes=VMEM_LIMIT),
        name="mix_out",
    )(ya, zr, zr, zr, x2, gate, g_v, wts["w_s"], bs_mat, wts["w_out"])


def _expand_matrix():
    j = np.arange(LANES)[:, None]
    h = np.arange(SSD_W)[None, :] // HEAD_DIM
    return jnp.asarray((j % HEADS == h) & (j < 3 * HEADS), dtype=BF16)


def _stream_layer(x2, wts, p, layer, mod_rows, h0f, h0b, *, n_seq, seq_len, row_len, tm_in, tm_out):
    d = x2.shape[1]
    shift = mod_rows[:, None, 0:d]
    gs = p["g_pre"] * (1.0 + mod_rows[:, None, d:2 * d])
    gate = mod_rows[:, None, 2 * d:3 * d] * p["g_post"]
    zx, zr, dt = _inproj(x2, gs, shift, wts, layer, seq_len=seq_len, row_len=row_len, tm=tm_in)
    ck = min(SCAN_CHUNKS, seq_len // CHUNK)
    hb_starts, hb_fin, cum, rowp, vals = _bwd_scan(zx, dt, p["bias"], p["a_neg"], p["e_mat"], h0b,
                                                   n_seq=n_seq, seq_len=seq_len, ck=ck)
    ya, hf_fin = _ssd(zx, zr, cum, rowp, vals, p["e_mat"], h0f, hb_starts,
                      p["dskip"], n_seq=n_seq, seq_len=seq_len, ck=ck)
    x_new = _out(ya, zr, x2, gate, p["g_v"], p["bs_mat"], wts, layer,
                 seq_len=seq_len, tm=tm_out)
    return x_new, hf_fin, hb_fin


def _all_layer_weights(w_in, conv_w, conv_b, g_ssd, g_mlp, w_s, w_out):
    o = XBC_W + DT_W
    out_gain = jnp.concatenate([g_ssd, g_mlp], axis=1)[:, :, None]
    w16 = lax.optimization_barrier(jnp.swapaxes(w_in, 1, 2).astype(BF16))
    return {
        "w_conv": w16,
        "w_plain": w16.reshape(-1, w16.shape[2]),
        "w_dt": w16,
        "cw8": jnp.pad(conv_w, ((0, 0), (0, 8 - CONV_W), (0, 0))),
        "cb": conv_b[:, None, :],
        "w_s": w_s.astype(BF16),
        "w_out": (w_out * out_gain).astype(BF16),
    }


def _layer_params(l, g_pre, g_post, dt_bias, a_log, d_skip, g_v, b_s, e_mat):
    pad = jnp.zeros((LANES - DT_W,), F32)
    return {
        "g_pre": g_pre[l][None, None, :],
        "g_post": g_post[l][None, :],
        "bias": jnp.concatenate([dt_bias[l].reshape(-1), pad])[None, :],
        "a_neg": jnp.concatenate([-jnp.exp(a_log[l].reshape(-1)), pad])[None, :],
        "dskip": jnp.repeat(d_skip[l], HEAD_DIM)[None, :],
        "g_v": g_v[l][None, :],
        "bs_mat": jnp.repeat(b_s[l].T, MLP_GROUP_DIM, axis=1),
        "e_mat": e_mat,
    }


def kernel(x, c, ctx, c_ctx, w_ada, b_ada, g_pre, g_post, w_in, conv_w, conv_b, dt_bias, a_log,
           d_skip, g_ssd, g_v, w_s, b_s, g_mlp, w_out):
    bsz, seq, d = x.shape
    ctx_len = ctx.shape[1]
    depth = w_in.shape[0]
    assert seq % LATENT_TM_IN == 0 and ctx_len % CHUNK == 0 and ctx_len & (ctx_len - 1) == 0
    assert w_in.shape[2] == MAIN_W + DT_W and bsz + 1 <= 8

    cc = jnp.concatenate([c, c_ctx[None, :], jnp.zeros((8 - bsz - 1, d), F32)], axis=0)
    mod = _modulation(cc, w_ada, b_ada)
    e_mat = _expand_matrix()
    wts = _all_layer_weights(w_in, conv_w, conv_b, g_ssd, g_mlp, w_s, w_out)

    xs = x.reshape(bsz * seq, d)
    cs = ctx.reshape(bsz * ctx_len, d)
    zeros_h = jnp.zeros((bsz, GN, GROUP_W), F32)
    for l in range(depth):
        p = _layer_params(l, g_pre, g_post, dt_bias, a_log, d_skip, g_v, b_s, e_mat)
        cs, h_f, h_b = _stream_layer(cs, wts, p, l, mod[l, bsz:bsz + 1], zeros_h, zeros_h, n_seq=bsz,
                                     seq_len=ctx_len, row_len=ctx_len, tm_in=bsz * ctx_len, tm_out=ctx_len)
        xs, _, _ = _stream_layer(xs, wts, p, l, mod[l, :bsz], h_f, h_b, n_seq=bsz, seq_len=seq,
                                 row_len=GRID_W, tm_in=LATENT_TM_IN, tm_out=LATENT_TM_OUT)
    return xs.reshape(bsz, seq, d)
```

```python
import functools

import jax
import jax.numpy as jnp
import numpy as np
from jax import lax
from jax.experimental import pallas as pl
from jax.experimental.pallas import tpu as pltpu

F32 = jnp.float32
BF16 = jnp.bfloat16

GRID_W = 64
HEADS = 32
HEAD_DIM = 64
GROUPS = 8
HEADS_PER_GROUP = HEADS // GROUPS
STATE = 128
CHUNK = 128
CONV_W = 5
CONV_PAD = CONV_W // 2
MLP_GROUPS = 16
MLP_GROUP_DIM = 128
MLP_CHUNK = 128
EPS = 1e-6
LOG2E = 1.4426950408889634

SSD_W = HEADS * HEAD_DIM
GN = GROUPS * STATE
XBC_W = SSD_W + 2 * GN
DT_W = 2 * HEADS
MLP_W = MLP_GROUPS * MLP_GROUP_DIM
GROUP_W = HEADS_PER_GROUP * HEAD_DIM
REST_W = SSD_W + 3 * MLP_W
MAIN_W = XBC_W + REST_W
INPROJ_STEPS = 8
CONV_SLAB = XBC_W // INPROJ_STEPS
PLAIN_SLAB = REST_W // INPROJ_STEPS
LANES = 128
SCAN_CHUNKS = 4
LATENT_TM_IN = 1024
LATENT_TM_OUT = 512

VMEM_LIMIT = 56 * 1024 * 1024


def _silu(v):
    h = 0.5 * v
    return h * (1.0 + jnp.tanh(h))


def _rms(v):
    return v * lax.rsqrt(jnp.mean(v * v, axis=-1, keepdims=True) + EPS)


def _dot_nt(a, b):
    return lax.dot_general(a, b, (((1,), (1,)), ((), ())), preferred_element_type=F32)


def _split3(v):
    v1 = v.astype(BF16)
    r1 = v - v1.astype(F32)
    v2 = r1.astype(BF16)
    r2 = r1 - v2.astype(F32)
    return v1, v2, r2.astype(BF16)


def _mod_kernel(cc_ref, w_ref, b_ref, o_ref):
    sc = _silu(cc_ref[...])
    o_ref[0] = jnp.dot(sc, w_ref[0], preferred_element_type=F32) + b_ref[0]


def _modulation(cc, w_ada, b_ada):
    depth, d, d3 = w_ada.shape
    tn = 768
    return pl.pallas_call(
        _mod_kernel,
        grid=(depth, d3 // tn),
        in_specs=[
            pl.BlockSpec((8, d), lambda l, j: (0, 0)),
            pl.BlockSpec((1, d, tn), lambda l, j: (l, 0, j)),
            pl.BlockSpec((1, 1, tn), lambda l, j: (l, 0, j)),
        ],
        out_specs=pl.BlockSpec((1, 8, tn), lambda l, j: (l, 0, j)),
        out_shape=jax.ShapeDtypeStruct((depth, 8, d3), F32),
        compiler_params=pltpu.CompilerParams(
            dimension_semantics=("parallel", "parallel"), vmem_limit_bytes=VMEM_LIMIT),
        name="adaln_mod",
    )(cc, w_ada, b_ada.reshape(depth, 1, d3))


def _inproj_kernel(x_ref, gs_ref, sh_ref, wc_ref, wp_ref, wdt_ref, cw_ref, cb_ref,
                   zx_ref, zr_ref, dt_ref, hx_ref, *, row_len):
    j = pl.program_id(1)
    tm, tc = zx_ref.shape

    @pl.when(j == 0)
    def _():
        hb = (_rms(x_ref[...]) * gs_ref[0] + sh_ref[0]).astype(BF16)
        hx_ref[...] = hb
        dt_ref[...] = _dot_nt(hb, wdt_ref[...])

    hx = hx_ref[...]
    acc = _dot_nt(hx, wc_ref[...])
    zr_ref[...] = _dot_nt(hx, wp_ref[...]).astype(BF16)

    rows3 = acc.reshape(tm // row_len, row_len, tc)
    y = cb_ref[...] + cw_ref[CONV_PAD:CONV_PAD + 1, :] * acc
    for d in (-2, -1, 1, 2):
        pad = jnp.zeros((tm // row_len, abs(d), tc), F32)
        if d < 0:
            shifted = jnp.concatenate([pad, rows3[:, :row_len + d, :]], axis=1)
        else:
            shifted = jnp.concatenate([rows3[:, d:, :], pad], axis=1)
        y = y + cw_ref[CONV_PAD + d:CONV_PAD + d + 1, :] * shifted.reshape(tm, tc)
    zx_ref[...] = _silu(y).astype(BF16)


def _inproj(x2, gs, shift, wts, layer, *, seq_len, row_len, tm):
    tokens, d = x2.shape
    assert gs.shape[0] == 1 or seq_len % tm == 0
    mod_row = (lambda i: 0) if gs.shape[0] == 1 else (lambda i: (i * tm) // seq_len)
    kern = functools.partial(_inproj_kernel, row_len=row_len)
    return pl.pallas_call(
        kern,
        grid=(tokens // tm, INPROJ_STEPS),
        in_specs=[
            pl.BlockSpec((tm, d), lambda i, j: (i, 0)),
            pl.BlockSpec((1, 1, d), lambda i, j: (mod_row(i), 0, 0)),
            pl.BlockSpec((1, 1, d), lambda i, j: (mod_row(i), 0, 0)),
            pl.BlockSpec((None, CONV_SLAB, d), lambda i, j: (layer, j, 0)),
            pl.BlockSpec((pl.Element(PLAIN_SLAB), pl.Element(d)),
                         lambda i, j: (pl.multiple_of(
                             layer * (MAIN_W + DT_W) + XBC_W + DT_W + j * PLAIN_SLAB, DT_W), 0)),
            pl.BlockSpec((None, LANES, d), lambda i, j: (layer, XBC_W // LANES, 0)),
            pl.BlockSpec((None, 8, CONV_SLAB), lambda i, j: (layer, 0, j)),
            pl.BlockSpec((None, 1, CONV_SLAB), lambda i, j: (layer, 0, j)),
        ],
        out_specs=[
            pl.BlockSpec((tm, CONV_SLAB), lambda i, j: (i, j)),
            pl.BlockSpec((tm, PLAIN_SLAB), lambda i, j: (i, j)),
            pl.BlockSpec((tm, LANES), lambda i, j: (i, 0)),
        ],
        out_shape=[
            jax.ShapeDtypeStruct((tokens, XBC_W), BF16),
            jax.ShapeDtypeStruct((tokens, REST_W), BF16),
            jax.ShapeDtypeStruct((tokens, LANES), F32),
        ],
        scratch_shapes=[pltpu.VMEM((tm, d), BF16)],
        compiler_params=pltpu.CompilerParams(
            dimension_semantics=("parallel", "arbitrary"), vmem_limit_bytes=VMEM_LIMIT),
        name="inproj",
    )(x2, gs, shift, wts["w_conv"], wts["w_plain"], wts["w_dt"], wts["cw8"], wts["cb"])


def _chunk_prep(dtr, bias, a_neg):
    q = dtr.shape[0]
    xx = dtr + bias
    dt = jnp.maximum(xx, 0.0) + jnp.log1p(jnp.exp(-jnp.abs(xx)))
    a = dt * a_neg * LOG2E
    acat = jnp.concatenate(_split3(a), axis=0)
    row = lax.broadcasted_iota(jnp.int32, (q, q), 0)
    col = lax.broadcasted_iota(jnp.int32, (q, q), 1)
    t_inc = jnp.where(col <= row, 1.0, 0.0).astype(BF16)
    t_suf = jnp.where(col >= row, 1.0, 0.0).astype(BF16)
    incl = jnp.dot(jnp.concatenate([t_inc] * 3, axis=1), acat, preferred_element_type=F32)
    suf = jnp.dot(jnp.concatenate([t_suf] * 3, axis=1), acat, preferred_element_type=F32)
    lane = lax.broadcasted_iota(jnp.int32, (q, LANES), 1)
    fwd = lane < HEADS
    cum = jnp.where(fwd, incl, suf)
    tot = jnp.where(fwd, incl[q - 1:q, :], suf[0:1, :])
    w = dt * jnp.exp2(tot - cum)
    lo = lane < DT_W
    dt_sum = dt + pltpu.roll(dt, LANES - HEADS, axis=1)
    rowp = jnp.where(lo, cum - jnp.log2(dt), pltpu.roll(jnp.log2(dt_sum), DT_W, axis=1))
    vals = jnp.where(lo, jnp.exp2(cum), pltpu.roll(w, DT_W, axis=1))
    return cum, rowp, vals


def _expand(vals, wanted, e_ref):
    v1 = vals.astype(BF16).astype(F32)
    r1 = vals - v1
    v2 = r1.astype(BF16).astype(F32)
    terms = (v1, v2, r1 - v2)
    packed = []
    for which, rows in wanted:
        n = rows.stop - rows.start
        lane = lax.broadcasted_iota(jnp.int32, (n, LANES), 1)
        p = jnp.zeros((n, LANES), F32)
        for k in (2, 1, 0):
            t = terms[k][rows]
            shift = (HEADS * (k - which)) % LANES
            t = t if shift == 0 else pltpu.roll(t, shift, axis=1)
            p = jnp.where(lane < HEADS * (k + 1), t, p)
        packed.append(p.astype(BF16))
    stacked = packed[0] if len(packed) == 1 else jnp.concatenate(packed, axis=0)
    return jnp.dot(stacked, e_ref[...], preferred_element_type=F32)


def _state_outer(b_g, xw_g):
    return lax.dot_general(b_g, xw_g, (((0,), (0,)), ((), ())), preferred_element_type=F32)


def _bwd_kernel(x_ref, b_ref, dt_ref, bias_ref, a_ref, e_ref, h0_ref,
                hs_ref, hfin_ref, cum_ref, rowp_ref, vals_ref, st_ref, *, ck):
    s = pl.program_id(1)

    @pl.when(s == 0)
    def _():
        st_ref[...] = h0_ref[0]

    q = CHUNK
    for ci in reversed(range(ck)):
        tok = slice(ci * q, (ci + 1) * q)
        cum, rowp, vals = _chunk_prep(dt_ref[tok, :], bias_ref[...], a_ref[...])
        cum_ref[tok, :] = cum
        rowp_ref[tok, :] = rowp
        vals_ref[tok, :] = vals
        ex = _expand(vals, [(3, slice(0, q)), (1, slice(0, 16))], e_ref)
        hs_ref[0, ci] = st_ref[...].astype(BF16)
        for g in range(GROUPS):
            rows = slice(g * STATE, (g + 1) * STATE)
            cols = slice(g * GROUP_W, (g + 1) * GROUP_W)
            xw = (x_ref[tok, cols].astype(F32) * ex[0:q, cols]).astype(BF16)
            dec = ex[q:q + 1, cols]
            st_ref[rows, :] = st_ref[rows, :] * dec + _state_outer(b_ref[tok, rows], xw)

    @pl.when(s == pl.num_programs(1) - 1)
    def _():
        hfin_ref[0] = st_ref[...]


def _bwd_scan(zx, dt, bias, a_neg, e_mat, h0, *, n_seq, seq_len, ck):
    ns = seq_len // (ck * CHUNK)
    nc = seq_len // CHUNK
    tm = ck * CHUNK
    tok = lambda b, s: b * ns + (ns - 1 - s)
    return pl.pallas_call(
        functools.partial(_bwd_kernel, ck=ck),
        grid=(n_seq, ns),
        in_specs=[
            pl.BlockSpec((tm, SSD_W), lambda b, s: (tok(b, s), 0)),
            pl.BlockSpec((tm, GN), lambda b, s: (tok(b, s), SSD_W // GN)),
            pl.BlockSpec((tm, LANES), lambda b, s: (tok(b, s), 0)),
            pl.BlockSpec((1, LANES), lambda b, s: (0, 0)),
            pl.BlockSpec((1, LANES), lambda b, s: (0, 0)),
            pl.BlockSpec((LANES, SSD_W), lambda b, s: (0, 0)),
            pl.BlockSpec((1, GN, GROUP_W), lambda b, s: (b, 0, 0)),
        ],
        out_specs=[
            pl.BlockSpec((1, ck, GN, GROUP_W), lambda b, s: (b, ns - 1 - s, 0, 0)),
            pl.BlockSpec((1, GN, GROUP_W), lambda b, s: (b, 0, 0)),
            pl.BlockSpec((tm, LANES), lambda b, s: (tok(b, s), 0)),
            pl.BlockSpec((tm, LANES), lambda b, s: (tok(b, s), 0)),
            pl.BlockSpec((tm, LANES), lambda b, s: (tok(b, s), 0)),
        ],
        out_shape=[
            jax.ShapeDtypeStruct((n_seq, nc, GN, GROUP_W), BF16),
            jax.ShapeDtypeStruct((n_seq, GN, GROUP_W), F32),
            jax.ShapeDtypeStruct((n_seq * seq_len, LANES), F32),
            jax.ShapeDtypeStruct((n_seq * seq_len, LANES), F32),
            jax.ShapeDtypeStruct((n_seq * seq_len, LANES), F32),
        ],
        scratch_shapes=[pltpu.VMEM((GN, GROUP_W), F32)],
        compiler_params=pltpu.CompilerParams(
            dimension_semantics=("parallel", "arbitrary"), vmem_limit_bytes=VMEM_LIMIT),
        name="ssd_bwd_scan",
    )(zx, zx, dt, bias, a_neg, e_mat, h0)


def _ssd_kernel(x_ref, b_ref, c_ref, zs_ref, cum_ref, rowp_ref, vals_ref, e_ref, h0_ref, hb_ref,
                dskip_ref, ya_ref, hfin_ref, st_ref, y_ref, *, ck):
    s = pl.program_id(1)
    q = CHUNK

    @pl.when(s == 0)
    def _():
        st_ref[...] = h0_ref[0]

    row = lax.broadcasted_iota(jnp.int32, (q, q), 0)
    col = lax.broadcasted_iota(jnp.int32, (q, q), 1)
    lower = col < row
    upper = col > row
    first_head = lax.broadcasted_iota(jnp.int32, (q, 2 * HEAD_DIM), 1) < HEAD_DIM

    for ci in range(ck):
        tok = slice(ci * q, (ci + 1) * q)
        cum, vals = cum_ref[tok, :], vals_ref[tok, :]
        row_t = rowp_ref[tok, :].T
        ex = _expand(vals, [(0, slice(0, q)), (1, slice(0, q)), (2, slice(0, q))], e_ref)

        for g in range(GROUPS):
            rows = slice(g * STATE, (g + 1) * STATE)
            cols = slice(g * GROUP_W, (g + 1) * GROUP_W)
            x_g = x_ref[tok, cols]
            xf_g = x_g.astype(F32)
            b_g = b_ref[tok, rows]
            c_g = c_ref[tok, rows]
            cb = lax.dot_general(c_g, b_g, (((1,), (1,)), ((), ())), preferred_element_type=F32)
            ms = []
            for r in range(HEADS_PER_GROUP):
                h = g * HEADS_PER_GROUP + r
                cum_f = jnp.broadcast_to(cum[:, h:h + 1], (q, q))
                cum_b = jnp.broadcast_to(cum[:, HEADS + h:HEADS + h + 1], (q, q))
                arg = jnp.where(lower, cum_f - row_t[h:h + 1, :],
                                jnp.where(upper, cum_b - row_t[HEADS + h:HEADS + h + 1, :],
                                          row_t[2 * HEADS + h:2 * HEADS + h + 1, :]))
                ms.append((cb * jnp.exp2(arg)).astype(BF16))
            ys = []
            for pr in range(HEADS_PER_GROUP // 2):
                xp = x_g[:, pr * 2 * HEAD_DIM:(pr + 1) * 2 * HEAD_DIM]
                zero = jnp.zeros_like(xp)
                rhs = jnp.concatenate([jnp.where(first_head, xp, zero), jnp.where(first_head, zero, xp)], axis=0)
                lhs = jnp.concatenate([ms[2 * pr], ms[2 * pr + 1]], axis=1)
                ys.append(jnp.dot(lhs, rhs, preferred_element_type=F32))
            y_diag = jnp.concatenate(ys, axis=1)
            e_f = ex[0:q, cols]
            y_off = (jnp.dot(c_g, st_ref[rows, :].astype(BF16), preferred_element_type=F32) * e_f
                     + jnp.dot(c_g, hb_ref[0, ci, rows, :], preferred_element_type=F32) * ex[q:2 * q, cols])
            y_ref[tok, cols] = y_diag + y_off + dskip_ref[:, cols] * xf_g
            xw = (xf_g * ex[2 * q:3 * q, cols]).astype(BF16)
            st_ref[rows, :] = st_ref[rows, :] * e_f[q - 1:q, :] + _state_outer(b_g, xw)

        gated = y_ref[tok, :] * _silu(zs_ref[tok, :].astype(F32))
        ya_ref[tok, :] = _rms(gated).astype(BF16)

    @pl.when(s == pl.num_programs(1) - 1)
    def _():
        hfin_ref[0] = st_ref[...]


def _ssd(zx, zr, cum, rowp, vals, e_mat, h0f, hb_starts, dskip, *, n_seq, seq_len, ck):
    ns = seq_len // (ck * CHUNK)
    tm = ck * CHUNK
    tok = lambda b, s: b * ns + s
    tokens = n_seq * seq_len
    return pl.pallas_call(
        functools.partial(_ssd_kernel, ck=ck),
        grid=(n_seq, ns),
        in_specs=[
            pl.BlockSpec((tm, SSD_W), lambda b, s: (tok(b, s), 0)),
            pl.BlockSpec((tm, GN), lambda b, s: (tok(b, s), SSD_W // GN)),
            pl.BlockSpec((tm, GN), lambda b, s: (tok(b, s), SSD_W // GN + 1)),
            pl.BlockSpec((tm, SSD_W), lambda b, s: (tok(b, s), 0)),
            pl.BlockSpec((tm, LANES), lambda b, s: (tok(b, s), 0)),
            pl.BlockSpec((tm, LANES), lambda b, s: (tok(b, s), 0)),
            pl.BlockSpec((tm, LANES), lambda b, s: (tok(b, s), 0)),
            pl.BlockSpec((LANES, SSD_W), lambda b, s: (0, 0)),
            pl.BlockSpec((1, GN, GROUP_W), lambda b, s: (b, 0, 0)),
            pl.BlockSpec((1, ck, GN, GROUP_W), lambda b, s: (b, s, 0, 0)),
            pl.BlockSpec((1, SSD_W), lambda b, s: (0, 0)),
        ],
        out_specs=[
            pl.BlockSpec((tm, SSD_W), lambda b, s: (tok(b, s), 0)),
            pl.BlockSpec((1, GN, GROUP_W), lambda b, s: (b, 0, 0)),
        ],
        out_shape=[
            jax.ShapeDtypeStruct((tokens, SSD_W), BF16),
            jax.ShapeDtypeStruct((n_seq, GN, GROUP_W), F32),
        ],
        scratch_shapes=[pltpu.VMEM((GN, GROUP_W), F32), pltpu.VMEM((tm, SSD_W), F32)],
        compiler_params=pltpu.CompilerParams(
            dimension_semantics=("parallel", "arbitrary"), vmem_limit_bytes=VMEM_LIMIT),
        name="ssd_main",
    )(zx, zx, zx, zr, cum, rowp, vals, e_mat, h0f, hb_starts, dskip)


def _out_kernel(ya_ref, u_ref, v_ref, zm_ref, x_ref, gate_ref, gv_ref,
                ws_ref, bs_ref, wout_ref, o_ref, sg_ref):
    tm = x_ref.shape[0]
    vn = (_rms(v_ref[...].astype(F32)) * gv_ref[...]).astype(BF16)
    for c in range(tm // MLP_CHUNK):
        rows = slice(c * MLP_CHUNK, (c + 1) * MLP_CHUNK)
        for g in range(MLP_GROUPS):
            cols = slice(g * MLP_GROUP_DIM, (g + 1) * MLP_GROUP_DIM)
            sg_ref[rows, cols] = jnp.dot(ws_ref[g], vn[rows, cols], preferred_element_type=F32) + bs_ref[:, cols]
    yb = _rms(u_ref[...].astype(F32) * sg_ref[...] * _silu(zm_ref[...].astype(F32))).astype(BF16)
    mixed = (jnp.dot(ya_ref[...], wout_ref[0:SSD_W, :], preferred_element_type=F32)
             + jnp.dot(yb, wout_ref[SSD_W:, :], preferred_element_type=F32))
    o_ref[...] = x_ref[...] + gate_ref[0] * _rms(mixed)


def _out(ya, zr, x2, gate, g_v, bs_mat, wts, layer, *, seq_len, tm):
    tokens, d = x2.shape
    assert gate.shape[0] == 1 or seq_len % tm == 0
    mod_row = (lambda i: 0) if gate.shape[0] == 1 else (lambda i: (i * tm) // seq_len)
    col0 = SSD_W // MLP_W
    const2 = lambda i: (0, 0)
    return pl.pallas_call(
        _out_kernel,
        grid=(tokens // tm,),
        in_specs=[
            pl.BlockSpec((tm, SSD_W), lambda i: (i, 0)),
            pl.BlockSpec((tm, MLP_W), lambda i: (i, col0)),
            pl.BlockSpec((tm, MLP_W), lambda i: (i, col0 + 1)),
            pl.BlockSpec((tm, MLP_W), lambda i: (i, col0 + 2)),
            pl.BlockSpec((tm, d), lambda i: (i, 0)),
            pl.BlockSpec((1, 1, d), lambda i: (mod_row(i), 0, 0)),
            pl.BlockSpec((1, MLP_W), const2),
            pl.BlockSpec((None, MLP_GROUPS, MLP_CHUNK, MLP_CHUNK), lambda i: (layer, 0, 0, 0)),
            pl.BlockSpec((MLP_CHUNK, MLP_W), const2),
            pl.BlockSpec((None, SSD_W + MLP_W, d), lambda i: (layer, 0, 0), pipeline_mode=pl.Buffered(1)),
        ],
        out_specs=pl.BlockSpec((tm, d), lambda i: (i, 0)),
        out_shape=jax.ShapeDtypeStruct((tokens, d), F32),
        scratch_shapes=[pltpu.VMEM((tm, MLP_W), F32)],
        compiler_params=pltpu.CompilerParams(
            dimension_semantics=("parallel",), vmem_limit_bytes=VMEM_LIMIT),
        name="mix_out",
    )(ya, zr, zr, zr, x2, gate, g_v, wts["w_s"], bs_mat, wts["w_out"])


def _expand_matrix():
    j = np.arange(LANES)[:, None]
    h = np.arange(SSD_W)[None, :] // HEAD_DIM
    return jnp.asarray((j % HEADS == h) & (j < 3 * HEADS), dtype=BF16)


def _stream_layer(x2, wts, p, layer, mod_rows, h0f, h0b, *, n_seq, seq_len, row_len, tm_in, tm_out):
    d = x2.shape[1]
    shift = mod_rows[:, None, 0:d]
    gs = p["g_pre"] * (1.0 + mod_rows[:, None, d:2 * d])
    gate = mod_rows[:, None, 2 * d:3 * d] * p["g_post"]
    zx, zr, dt = _inproj(x2, gs, shift, wts, layer, seq_len=seq_len, row_len=row_len, tm=tm_in)
    ck = min(SCAN_CHUNKS, seq_len // CHUNK)
    hb_starts, hb_fin, cum, rowp, vals = _bwd_scan(zx, dt, p["bias"], p["a_neg"], p["e_mat"], h0b,
                                                   n_seq=n_seq, seq_len=seq_len, ck=ck)
    ya, hf_fin = _ssd(zx, zr, cum, rowp, vals, p["e_mat"], h0f, hb_starts,
                      p["dskip"], n_seq=n_seq, seq_len=seq_len, ck=ck)
    x_new = _out(ya, zr, x2, gate, p["g_v"], p["bs_mat"], wts, layer,
                 seq_len=seq_len, tm=tm_out)
    return x_new, hf_fin, hb_fin


def _all_layer_weights(w_in, conv_w, conv_b, g_ssd, g_mlp, w_s, w_out):
    o = XBC_W + DT_W
    out_gain = jnp.concatenate([g_ssd, g_mlp], axis=1)[:, :, None]
    w16 = lax.optimization_barrier(jnp.swapaxes(w_in, 1, 2).astype(BF16))
    return {
        "w_conv": w16,
        "w_plain": w16.reshape(-1, w16.shape[2]),
        "w_dt": w16,
        "cw8": jnp.pad(conv_w, ((0, 0), (0, 8 - CONV_W), (0, 0))),
        "cb": conv_b[:, None, :],
        "w_s": w_s.astype(BF16),
        "w_out": (w_out * out_gain).astype(BF16),
    }


def _layer_params(l, g_pre, g_post, dt_bias, a_log, d_skip, g_v, b_s, e_mat):
    pad = jnp.zeros((LANES - DT_W,), F32)
    return {
        "g_pre": g_pre[l][None, None, :],
        "g_post": g_post[l][None, :],
        "bias": jnp.concatenate([dt_bias[l].reshape(-1), pad])[None, :],
        "a_neg": jnp.concatenate([-jnp.exp(a_log[l].reshape(-1)), pad])[None, :],
        "dskip": jnp.repeat(d_skip[l], HEAD_DIM)[None, :],
        "g_v": g_v[l][None, :],
        "bs_mat": jnp.repeat(b_s[l].T, MLP_GROUP_DIM, axis=1),
        "e_mat": e_mat,
    }


def kernel(x, c, ctx, c_ctx, w_ada, b_ada, g_pre, g_post, w_in, conv_w, conv_b, dt_bias, a_log,
           d_skip, g_ssd, g_v, w_s, b_s, g_mlp, w_out):
    bsz, seq, d = x.shape
    ctx_len = ctx.shape[1]
    depth = w_in.shape[0]
    assert seq % LATENT_TM_IN == 0 and ctx_len % CHUNK == 0 and ctx_len & (ctx_len - 1) == 0
    assert w_in.shape[2] == MAIN_W + DT_W and bsz + 1 <= 8

    cc = jnp.concatenate([c, c_ctx[None, :], jnp.zeros((8 - bsz - 1, d), F32)], axis=0)
    mod = _modulation(cc, w_ada, b_ada)
    e_mat = _expand_matrix()
    wts = _all_layer_weights(w_in, conv_w, conv_b, g_ssd, g_mlp, w_s, w_out)

    xs = x.reshape(bsz * seq, d)
    cs = ctx.reshape(bsz * ctx_len, d)
    zeros_h = jnp.zeros((bsz, GN, GROUP_W), F32)
    for l in range(depth):
        p = _layer_params(l, g_pre, g_post, dt_bias, a_log, d_skip, g_v, b_s, e_mat)
        cs, h_f, h_b = _stream_layer(cs, wts, p, l, mod[l, bsz:bsz + 1], zeros_h, zeros_h, n_seq=bsz,
                                     seq_len=ctx_len, row_len=ctx_len, tm_in=bsz * ctx_len, tm_out=ctx_len)
        xs, _, _ = _stream_layer(xs, wts, p, l, mod[l, :bsz], h_f, h_b, n_seq=bsz, seq_len=seq,
                                 row_len=GRID_W, tm_in=LATENT_TM_IN, tm_out=LATENT_TM_OUT)
    return xs.reshape(bsz, seq, d)
```

```python
import functools

import jax
import jax.numpy as jnp
import numpy as np
from jax import lax
from jax.experimental import pallas as pl
from jax.experimental.pallas import tpu as pltpu

F32 = jnp.float32
BF16 = jnp.bfloat16

GRID_W = 64
HEADS = 32
HEAD_DIM = 64
GROUPS = 8
HEADS_PER_GROUP = HEADS // GROUPS
STATE = 128
CHUNK = 128
CONV_W = 5
CONV_PAD = CONV_W // 2
MLP_GROUPS = 16
MLP_GROUP_DIM = 128
MLP_CHUNK = 128
EPS = 1e-6
LOG2E = 1.4426950408889634

SSD_W = HEADS * HEAD_DIM
GN = GROUPS * STATE
XBC_W = SSD_W + 2 * GN
DT_W = 2 * HEADS
MLP_W = MLP_GROUPS * MLP_GROUP_DIM
GROUP_W = HEADS_PER_GROUP * HEAD_DIM
REST_W = SSD_W + 3 * MLP_W
MAIN_W = XBC_W + REST_W
INPROJ_STEPS = 8
CONV_SLAB = XBC_W // INPROJ_STEPS
PLAIN_SLAB = REST_W // INPROJ_STEPS
LANES = 128
SCAN_CHUNKS = 4
LATENT_TM_IN = 1024
LATENT_TM_OUT = 512

VMEM_LIMIT = 56 * 1024 * 1024


def _silu(v):
    h = 0.5 * v
    return h * (1.0 + jnp.tanh(h))


def _rms(v):
    return v * lax.rsqrt(jnp.mean(v * v, axis=-1, keepdims=True) + EPS)


def _dot_nt(a, b):
    return lax.dot_general(a, b, (((1,), (1,)), ((), ())), preferred_element_type=F32)


def _split3(v):
    v1 = v.astype(BF16)
    r1 = v - v1.astype(F32)
    v2 = r1.astype(BF16)
    r2 = r1 - v2.astype(F32)
    return v1, v2, r2.astype(BF16)


def _mod_kernel(cc_ref, w_ref, b_ref, o_ref):
    sc = _silu(cc_ref[...])
    o_ref[0] = jnp.dot(sc, w_ref[0], preferred_element_type=F32) + b_ref[0]


def _modulation(cc, w_ada, b_ada):
    depth, d, d3 = w_ada.shape
    tn = 768
    return pl.pallas_call(
        _mod_kernel,
        grid=(depth, d3 // tn),
        in_specs=[
            pl.BlockSpec((8, d), lambda l, j: (0, 0)),
            pl.BlockSpec((1, d, tn), lambda l, j: (l, 0, j)),
            pl.BlockSpec((1, 1, tn), lambda l, j: (l, 0, j)),
        ],
        out_specs=pl.BlockSpec((1, 8, tn), lambda l, j: (l, 0, j)),
        out_shape=jax.ShapeDtypeStruct((depth, 8, d3), F32),
        compiler_params=pltpu.CompilerParams(
            dimension_semantics=("parallel", "parallel"), vmem_limit_bytes=VMEM_LIMIT),
        name="adaln_mod",
    )(cc, w_ada, b_ada.reshape(depth, 1, d3))


def _inproj_kernel(x_ref, gs_ref, sh_ref, wc_ref, wp_ref, wdt_ref, cw_ref, cb_ref,
                   zx_ref, zr_ref, dt_ref, hx_ref, *, row_len):
    j = pl.program_id(1)
    tm, tc = zx_ref.shape

    @pl.when(j == 0)
    def _():
        hb = (_rms(x_ref[...]) * gs_ref[0] + sh_ref[0]).astype(BF16)
        hx_ref[...] = hb
        dt_ref[...] = _dot_nt(hb, wdt_ref[...])

    hx = hx_ref[...]
    acc = _dot_nt(hx, wc_ref[...])
    zr_ref[...] = _dot_nt(hx, wp_ref[...]).astype(BF16)

    rows3 = acc.reshape(tm // row_len, row_len, tc)
    y = cb_ref[...] + cw_ref[CONV_PAD:CONV_PAD + 1, :] * acc
    for d in (-2, -1, 1, 2):
        pad = jnp.zeros((tm // row_len, abs(d), tc), F32)
        if d < 0:
            shifted = jnp.concatenate([pad, rows3[:, :row_len + d, :]], axis=1)
        else:
            shifted = jnp.concatenate([rows3[:, d:, :], pad], axis=1)
        y = y + cw_ref[CONV_PAD + d:CONV_PAD + d + 1, :] * shifted.reshape(tm, tc)
    zx_ref[...] = _silu(y).astype(BF16)


def _inproj(x2, gs, shift, wts, layer, *, seq_len, row_len, tm):
    tokens, d = x2.shape
    assert gs.shape[0] == 1 or seq_len % tm == 0
    mod_row = (lambda i: 0) if gs.shape[0] == 1 else (lambda i: (i * tm) // seq_len)
    kern = functools.partial(_inproj_kernel, row_len=row_len)
    return pl.pallas_call(
        kern,
        grid=(tokens // tm, INPROJ_STEPS),
        in_specs=[
            pl.BlockSpec((tm, d), lambda i, j: (i, 0)),
            pl.BlockSpec((1, 1, d), lambda i, j: (mod_row(i), 0, 0)),
            pl.BlockSpec((1, 1, d), lambda i, j: (mod_row(i), 0, 0)),
            pl.BlockSpec((None, CONV_SLAB, d), lambda i, j: (layer, j, 0)),
            pl.BlockSpec((pl.Element(PLAIN_SLAB), pl.Element(d)),
                         lambda i, j: (pl.multiple_of(
                             layer * (MAIN_W + DT_W) + XBC_W + DT_W + j * PLAIN_SLAB, DT_W), 0)),
            pl.BlockSpec((None, LANES, d), lambda i, j: (layer, XBC_W // LANES, 0)),
            pl.BlockSpec((None, 8, CONV_SLAB), lambda i, j: (layer, 0, j)),
            pl.BlockSpec((None, 1, CONV_SLAB), lambda i, j: (layer, 0, j)),
        ],
        out_specs=[
            pl.BlockSpec((tm, CONV_SLAB), lambda i, j: (i, j)),
            pl.BlockSpec((tm, PLAIN_SLAB), lambda i, j: (i, j)),
            pl.BlockSpec((tm, LANES), lambda i, j: (i, 0)),
        ],
        out_shape=[
            jax.ShapeDtypeStruct((tokens, XBC_W), BF16),
            jax.ShapeDtypeStruct((tokens, REST_W), BF16),
            jax.ShapeDtypeStruct((tokens, LANES), F32),
        ],
        scratch_shapes=[pltpu.VMEM((tm, d), BF16)],
        compiler_params=pltpu.CompilerParams(
            dimension_semantics=("parallel", "arbitrary"), vmem_limit_bytes=VMEM_LIMIT),
        name="inproj",
    )(x2, gs, shift, wts["w_conv"], wts["w_plain"], wts["w_dt"], wts["cw8"], wts["cb"])


def _chunk_prep(dtr, bias, a_neg):
    q = dtr.shape[0]
    xx = dtr + bias
    dt = jnp.maximum(xx, 0.0) + jnp.log1p(jnp.exp(-jnp.abs(xx)))
    a = dt * a_neg * LOG2E
    acat = jnp.concatenate(_split3(a), axis=0)
    row = lax.broadcasted_iota(jnp.int32, (q, q), 0)
    col = lax.broadcasted_iota(jnp.int32, (q, q), 1)
    t_inc = jnp.where(col <= row, 1.0, 0.0).astype(BF16)
    t_suf = jnp.where(col >= row, 1.0, 0.0).astype(BF16)
    incl = jnp.dot(jnp.concatenate([t_inc] * 3, axis=1), acat, preferred_element_type=F32)
    suf = jnp.dot(jnp.concatenate([t_suf] * 3, axis=1), acat, preferred_element_type=F32)
    lane = lax.broadcasted_iota(jnp.int32, (q, LANES), 1)
    fwd = lane < HEADS
    cum = jnp.where(fwd, incl, suf)
    tot = jnp.where(fwd, incl[q - 1:q, :], suf[0:1, :])
    w = dt * jnp.exp2(tot - cum)
    lo = lane < DT_W
    dt_sum = dt + pltpu.roll(dt, LANES - HEADS, axis=1)
    rowp = jnp.where(lo, cum - jnp.log2(dt), pltpu.roll(jnp.log2(dt_sum), DT_W, axis=1))
    vals = jnp.where(lo, jnp.exp2(cum), pltpu.roll(w, DT_W, axis=1))
    return cum, rowp, vals


def _expand(vals, wanted, e_ref):
    v1 = vals.astype(BF16).astype(F32)
    r1 = vals - v1
    v2 = r1.astype(BF16).astype(F32)
    terms = (v1, v2, r1 - v2)
    packed = []
    for which, rows in wanted:
        n = rows.stop - rows.start
        lane = lax.broadcasted_iota(jnp.int32, (n, LANES), 1)
        p = jnp.zeros((n, LANES), F32)
        for k in (2, 1, 0):
            t = terms[k][rows]
            shift = (HEADS * (k - which)) % LANES
            t = t if shift == 0 else pltpu.roll(t, shift, axis=1)
            p = jnp.where(lane < HEADS * (k + 1), t, p)
        packed.append(p.astype(BF16))
    stacked = packed[0] if len(packed) == 1 else jnp.concatenate(packed, axis=0)
    return jnp.dot(stacked, e_ref[...], preferred_element_type=F32)


def _state_outer(b_g, xw_g):
    return lax.dot_general(b_g, xw_g, (((0,), (0,)), ((), ())), preferred_element_type=F32)


def _bwd_kernel(x_ref, b_ref, dt_ref, bias_ref, a_ref, e_ref, h0_ref,
                hs_ref, hfin_ref, cum_ref, rowp_ref, vals_ref, st_ref, *, ck):
    s = pl.program_id(1)

    @pl.when(s == 0)
    def _():
        st_ref[...] = h0_ref[0]

    q = CHUNK
    vals_all = []
    for ci in range(ck):
        tok = slice(ci * q, (ci + 1) * q)
        cum, rowp, vals = _chunk_prep(dt_ref[tok, :], bias_ref[...], a_ref[...])
        cum_ref[tok, :] = cum
        rowp_ref[tok, :] = rowp
        vals_ref[tok, :] = vals
        vals_all.append(vals)
    wanted = []
    for ci in range(ck):
        wanted += [(3, slice(ci * q, (ci + 1) * q)), (1, slice(ci * q, ci * q + 16))]
    ex_all = _expand(jnp.concatenate(vals_all, axis=0), wanted, e_ref)
    per_chunk = q + 16

    for ci in reversed(range(ck)):
        tok = slice(ci * q, (ci + 1) * q)
        base = ci * per_chunk
        hs_ref[0, ci] = st_ref[...].astype(BF16)
        for g in range(GROUPS):
            rows = slice(g * STATE, (g + 1) * STATE)
            cols = slice(g * GROUP_W, (g + 1) * GROUP_W)
            xw = (x_ref[tok, cols].astype(F32) * ex_all[base:base + q, cols]).astype(BF16)
            dec = ex_all[base + q:base + q + 1, cols]
            st_ref[rows, :] = st_ref[rows, :] * dec + _state_outer(b_ref[tok, rows], xw)

    @pl.when(s == pl.num_programs(1) - 1)
    def _():
        hfin_ref[0] = st_ref[...]


def _bwd_scan(zx, dt, bias, a_neg, e_mat, h0, *, n_seq, seq_len, ck):
    ns = seq_len // (ck * CHUNK)
    nc = seq_len // CHUNK
    tm = ck * CHUNK
    tok = lambda b, s: b * ns + (ns - 1 - s)
    return pl.pallas_call(
        functools.partial(_bwd_kernel, ck=ck),
        grid=(n_seq, ns),
        in_specs=[
            pl.BlockSpec((tm, SSD_W), lambda b, s: (tok(b, s), 0)),
            pl.BlockSpec((tm, GN), lambda b, s: (tok(b, s), SSD_W // GN)),
            pl.BlockSpec((tm, LANES), lambda b, s: (tok(b, s), 0)),
            pl.BlockSpec((1, LANES), lambda b, s: (0, 0)),
            pl.BlockSpec((1, LANES), lambda b, s: (0, 0)),
            pl.BlockSpec((LANES, SSD_W), lambda b, s: (0, 0)),
            pl.BlockSpec((1, GN, GROUP_W), lambda b, s: (b, 0, 0)),
        ],
        out_specs=[
            pl.BlockSpec((1, ck, GN, GROUP_W), lambda b, s: (b, ns - 1 - s, 0, 0)),
            pl.BlockSpec((1, GN, GROUP_W), lambda b, s: (b, 0, 0)),
            pl.BlockSpec((tm, LANES), lambda b, s: (tok(b, s), 0)),
            pl.BlockSpec((tm, LANES), lambda b, s: (tok(b, s), 0)),
            pl.BlockSpec((tm, LANES), lambda b, s: (tok(b, s), 0)),
        ],
        out_shape=[
            jax.ShapeDtypeStruct((n_seq, nc, GN, GROUP_W), BF16),
            jax.ShapeDtypeStruct((n_seq, GN, GROUP_W), F32),
            jax.ShapeDtypeStruct((n_seq * seq_len, LANES), F32),
            jax.ShapeDtypeStruct((n_seq * seq_len, LANES), F32),
            jax.ShapeDtypeStruct((n_seq * seq_len, LANES), F32),
        ],
        scratch_shapes=[pltpu.VMEM((GN, GROUP_W), F32)],
        compiler_params=pltpu.CompilerParams(
            dimension_semantics=("parallel", "arbitrary"), vmem_limit_bytes=VMEM_LIMIT),
        name="ssd_bwd_scan",
    )(zx, zx, dt, bias, a_neg, e_mat, h0)


def _ssd_kernel(x_ref, b_ref, c_ref, zs_ref, cum_ref, rowp_ref, vals_ref, e_ref, h0_ref, hb_ref,
                dskip_ref, ya_ref, hfin_ref, st_ref, y_ref, *, ck):
    s = pl.program_id(1)
    q = CHUNK

    @pl.when(s == 0)
    def _():
        st_ref[...] = h0_ref[0]

    row = lax.broadcasted_iota(jnp.int32, (q, q), 0)
    col = lax.broadcasted_iota(jnp.int32, (q, q), 1)
    lower = col < row
    upper = col > row
    first_head = lax.broadcasted_iota(jnp.int32, (q, 2 * HEAD_DIM), 1) < HEAD_DIM

    for ci in range(ck):
        tok = slice(ci * q, (ci + 1) * q)
        cum, vals = cum_ref[tok, :], vals_ref[tok, :]
        row_t = rowp_ref[tok, :].T
        ex = _expand(vals, [(0, slice(0, q)), (1, slice(0, q)), (2, slice(0, q))], e_ref)

        for g in range(GROUPS):
            rows = slice(g * STATE, (g + 1) * STATE)
            cols = slice(g * GROUP_W, (g + 1) * GROUP_W)
            x_g = x_ref[tok, cols]
            xf_g = x_g.astype(F32)
            b_g = b_ref[tok, rows]
            c_g = c_ref[tok, rows]
            cb = lax.dot_general(c_g, b_g, (((1,), (1,)), ((), ())), preferred_element_type=F32)
            ms = []
            for r in range(HEADS_PER_GROUP):
                h = g * HEADS_PER_GROUP + r
                cum_f = jnp.broadcast_to(cum[:, h:h + 1], (q, q))
                cum_b = jnp.broadcast_to(cum[:, HEADS + h:HEADS + h + 1], (q, q))
                arg = jnp.where(lower, cum_f - row_t[h:h + 1, :],
                                jnp.where(upper, cum_b - row_t[HEADS + h:HEADS + h + 1, :],
                                          row_t[2 * HEADS + h:2 * HEADS + h + 1, :]))
                ms.append((cb * jnp.exp2(arg)).astype(BF16))
            ys = []
            for pr in range(HEADS_PER_GROUP // 2):
                xp = x_g[:, pr * 2 * HEAD_DIM:(pr + 1) * 2 * HEAD_DIM]
                zero = jnp.zeros_like(xp)
                rhs = jnp.concatenate([jnp.where(first_head, xp, zero), jnp.where(first_head, zero, xp)], axis=0)
                lhs = jnp.concatenate([ms[2 * pr], ms[2 * pr + 1]], axis=1)
                ys.append(jnp.dot(lhs, rhs, preferred_element_type=F32))
            y_diag = jnp.concatenate(ys, axis=1)
            e_f = ex[0:q, cols]
            y_off = (jnp.dot(c_g, st_ref[rows, :].astype(BF16), preferred_element_type=F32) * e_f
                     + jnp.dot(c_g, hb_ref[0, ci, rows, :], preferred_element_type=F32) * ex[q:2 * q, cols])
            y_ref[tok, cols] = y_diag + y_off + dskip_ref[:, cols] * xf_g
            xw = (xf_g * ex[2 * q:3 * q, cols]).astype(BF16)
            st_ref[rows, :] = st_ref[rows, :] * e_f[q - 1:q, :] + _state_outer(b_g, xw)

        gated = y_ref[tok, :] * _silu(zs_ref[tok, :].astype(F32))
        ya_ref[tok, :] = _rms(gated).astype(BF16)

    @pl.when(s == pl.num_programs(1) - 1)
    def _():
        hfin_ref[0] = st_ref[...]


def _ssd(zx, zr, cum, rowp, vals, e_mat, h0f, hb_starts, dskip, *, n_seq, seq_len, ck):
    ns = seq_len // (ck * CHUNK)
    tm = ck * CHUNK
    tok = lambda b, s: b * ns + s
    tokens = n_seq * seq_len
    return pl.pallas_call(
        functools.partial(_ssd_kernel, ck=ck),
        grid=(n_seq, ns),
        in_specs=[
            pl.BlockSpec((tm, SSD_W), lambda b, s: (tok(b, s), 0)),
            pl.BlockSpec((tm, GN), lambda b, s: (tok(b, s), SSD_W // GN)),
            pl.BlockSpec((tm, GN), lambda b, s: (tok(b, s), SSD_W // GN + 1)),
            pl.BlockSpec((tm, SSD_W), lambda b, s: (tok(b, s), 0)),
            pl.BlockSpec((tm, LANES), lambda b, s: (tok(b, s), 0)),
            pl.BlockSpec((tm, LANES), lambda b, s: (tok(b, s), 0)),
            pl.BlockSpec((tm, LANES), lambda b, s: (tok(b, s), 0)),
            pl.BlockSpec((LANES, SSD_W), lambda b, s: (0, 0)),
            pl.BlockSpec((1, GN, GROUP_W), lambda b, s: (b, 0, 0)),
            pl.BlockSpec((1, ck, GN, GROUP_W), lambda b, s: (b, s, 0, 0)),
            pl.BlockSpec((1, SSD_W), lambda b, s: (0, 0)),
        ],
        out_specs=[
            pl.BlockSpec((tm, SSD_W), lambda b, s: (tok(b, s), 0)),
            pl.BlockSpec((1, GN, GROUP_W), lambda b, s: (b, 0, 0)),
        ],
        out_shape=[
            jax.ShapeDtypeStruct((tokens, SSD_W), BF16),
            jax.ShapeDtypeStruct((n_seq, GN, GROUP_W), F32),
        ],
        scratch_shapes=[pltpu.VMEM((GN, GROUP_W), F32), pltpu.VMEM((tm, SSD_W), F32)],
        compiler_params=pltpu.CompilerParams(
            dimension_semantics=("parallel", "arbitrary"), vmem_limit_bytes=VMEM_LIMIT),
        name="ssd_main",
    )(zx, zx, zx, zr, cum, rowp, vals, e_mat, h0f, hb_starts, dskip)


def _out_kernel(ya_ref, u_ref, v_ref, zm_ref, x_ref, gate_ref, gv_ref,
                ws_ref, bs_ref, wout_ref, o_ref, sg_ref):
    tm = x_ref.shape[0]
    vn = (_rms(v_ref[...].astype(F32)) * gv_ref[...]).astype(BF16)
    for c in range(tm // MLP_CHUNK):
        rows = slice(c * MLP_CHUNK, (c + 1) * MLP_CHUNK)
        for g in range(MLP_GROUPS):
            cols = slice(g * MLP_GROUP_DIM, (g + 1) * MLP_GROUP_DIM)
            sg_ref[rows, cols] = jnp.dot(ws_ref[g], vn[rows, cols], preferred_element_type=F32) + bs_ref[:, cols]
    yb = _rms(u_ref[...].astype(F32) * sg_ref[...] * _silu(zm_ref[...].astype(F32))).astype(BF16)
    mixed = (jnp.dot(ya_ref[...], wout_ref[0:SSD_W, :], preferred_element_type=F32)
             + jnp.dot(yb, wout_ref[SSD_W:, :], preferred_element_type=F32))
    o_ref[...] = x_ref[...] + gate_ref[0] * _rms(mixed)


def _out(ya, zr, x2, gate, g_v, bs_mat, wts, layer, *, seq_len, tm):
    tokens, d = x2.shape
    assert gate.shape[0] == 1 or seq_len % tm == 0
    mod_row = (lambda i: 0) if gate.shape[0] == 1 else (lambda i: (i * tm) // seq_len)
    col0 = SSD_W // MLP_W
    const2 = lambda i: (0, 0)
    return pl.pallas_call(
        _out_kernel,
        grid=(tokens // tm,),
        in_specs=[
            pl.BlockSpec((tm, SSD_W), lambda i: (i, 0)),
            pl.BlockSpec((tm, MLP_W), lambda i: (i, col0)),
            pl.BlockSpec((tm, MLP_W), lambda i: (i, col0 + 1)),
            pl.BlockSpec((tm, MLP_W), lambda i: (i, col0 + 2)),
            pl.BlockSpec((tm, d), lambda i: (i, 0)),
            pl.BlockSpec((1, 1, d), lambda i: (mod_row(i), 0, 0)),
            pl.BlockSpec((1, MLP_W), const2),
            pl.BlockSpec((None, MLP_GROUPS, MLP_CHUNK, MLP_CHUNK), lambda i: (layer, 0, 0, 0)),
            pl.BlockSpec((MLP_CHUNK, MLP_W), const2),
            pl.BlockSpec((None, SSD_W + MLP_W, d), lambda i: (layer, 0, 0), pipeline_mode=pl.Buffered(1)),
        ],
        out_specs=pl.BlockSpec((tm, d), lambda i: (i, 0)),
        out_shape=jax.ShapeDtypeStruct((tokens, d), F32),
        scratch_shapes=[pltpu.VMEM((tm, MLP_W), F32)],
        compiler_params=pltpu.CompilerParams(
            dimension_semantics=("parallel",), vmem_limit_bytes=VMEM_LIMIT),
        name="mix_out",
    )(ya, zr, zr, zr, x2, gate, g_v, wts["w_s"], bs_mat, wts["w_out"])


def _expand_matrix():
    j = np.arange(LANES)[:, None]
    h = np.arange(SSD_W)[None, :] // HEAD_DIM
    return jnp.asarray((j % HEADS == h) & (j < 3 * HEADS), dtype=BF16)


def _stream_layer(x2, wts, p, layer, mod_rows, h0f, h0b, *, n_seq, seq_len, row_len, tm_in, tm_out):
    d = x2.shape[1]
    shift = mod_rows[:, None, 0:d]
    gs = p["g_pre"] * (1.0 + mod_rows[:, None, d:2 * d])
    gate = mod_rows[:, None, 2 * d:3 * d] * p["g_post"]
    zx, zr, dt = _inproj(x2, gs, shift, wts, layer, seq_len=seq_len, row_len=row_len, tm=tm_in)
    ck = min(SCAN_CHUNKS, seq_len // CHUNK)
    hb_starts, hb_fin, cum, rowp, vals = _bwd_scan(zx, dt, p["bias"], p["a_neg"], p["e_mat"], h0b,
                                                   n_seq=n_seq, seq_len=seq_len, ck=ck)
    ya, hf_fin = _ssd(zx, zr, cum, rowp, vals, p["e_mat"], h0f, hb_starts,
                      p["dskip"], n_seq=n_seq, seq_len=seq_len, ck=ck)
    x_new = _out(ya, zr, x2, gate, p["g_v"], p["bs_mat"], wts, layer,
                 seq_len=seq_len, tm=tm_out)
    return x_new, hf_fin, hb_fin


def _all_layer_weights(w_in, conv_w, conv_b, g_ssd, g_mlp, w_s, w_out):
    o = XBC_W + DT_W
    out_gain = jnp.concatenate([g_ssd, g_mlp], axis=1)[:, :, None]
    w16 = lax.optimization_barrier(jnp.swapaxes(w_in, 1, 2).astype(BF16))
    return {
        "w_conv": w16,
        "w_plain": w16.reshape(-1, w16.shape[2]),
        "w_dt": w16,
        "cw8": jnp.pad(conv_w, ((0, 0), (0, 8 - CONV_W), (0, 0))),
        "cb": conv_b[:, None, :],
        "w_s": w_s.astype(BF16),
        "w_out": (w_out * out_gain).astype(BF16),
    }


def _layer_params(l, g_pre, g_post, dt_bias, a_log, d_skip, g_v, b_s, e_mat):
    pad = jnp.zeros((LANES - DT_W,), F32)
    return {
        "g_pre": g_pre[l][None, None, :],
        "g_post": g_post[l][None, :],
        "bias": jnp.concatenate([dt_bias[l].reshape(-1), pad])[None, :],
        "a_neg": jnp.concatenate([-jnp.exp(a_log[l].reshape(-1)), pad])[None, :],
        "dskip": jnp.repeat(d_skip[l], HEAD_DIM)[None, :],
        "g_v": g_v[l][None, :],
        "bs_mat": jnp.repeat(b_s[l].T, MLP_GROUP_DIM, axis=1),
        "e_mat": e_mat,
    }


def kernel(x, c, ctx, c_ctx, w_ada, b_ada, g_pre, g_post, w_in, conv_w, conv_b, dt_bias, a_log,
           d_skip, g_ssd, g_v, w_s, b_s, g_mlp, w_out):
    bsz, seq, d = x.shape
    ctx_len = ctx.shape[1]
    depth = w_in.shape[0]
    assert seq % LATENT_TM_IN == 0 and ctx_len % CHUNK == 0 and ctx_len & (ctx_len - 1) == 0
    assert w_in.shape[2] == MAIN_W + DT_W and bsz + 1 <= 8

    cc = jnp.concatenate([c, c_ctx[None, :], jnp.zeros((8 - bsz - 1, d), F32)], axis=0)
    mod = _modulation(cc, w_ada, b_ada)
    e_mat = _expand_matrix()
    wts = _all_layer_weights(w_in, conv_w, conv_b, g_ssd, g_mlp, w_s, w_out)

    xs = x.reshape(bsz * seq, d)
    cs = ctx.reshape(bsz * ctx_len, d)
    zeros_h = jnp.zeros((bsz, GN, GROUP_W), F32)
    for l in range(depth):
        p = _layer_params(l, g_pre, g_post, dt_bias, a_log, d_skip, g_v, b_s, e_mat)
        cs, h_f, h_b = _stream_layer(cs, wts, p, l, mod[l, bsz:bsz + 1], zeros_h, zeros_h, n_seq=bsz,
                                     seq_len=ctx_len, row_len=ctx_len, tm_in=bsz * ctx_len, tm_out=ctx_len)
        xs, _, _ = _stream_layer(xs, wts, p, l, mod[l, :bsz], h_f, h_b, n_seq=bsz, seq_len=seq,
                                 row_len=GRID_W, tm_in=LATENT_TM_IN, tm_out=LATENT_TM_OUT)
    return xs.reshape(bsz, seq, d)
```

```python
import functools

import jax
import jax.numpy as jnp
import numpy as np
from jax import lax
from jax.experimental import pallas as pl
from jax.experimental.pallas import tpu as pltpu

F32 = jnp.float32
BF16 = jnp.bfloat16

GRID_W = 64
HEADS = 32
HEAD_DIM = 64
GROUPS = 8
HEADS_PER_GROUP = HEADS // GROUPS
STATE = 128
CHUNK = 128
CONV_W = 5
CONV_PAD = CONV_W // 2
MLP_GROUPS = 16
MLP_GROUP_DIM = 128
MLP_CHUNK = 128
EPS = 1e-6
LOG2E = 1.4426950408889634

SSD_W = HEADS * HEAD_DIM
GN = GROUPS * STATE
XBC_W = SSD_W + 2 * GN
DT_W = 2 * HEADS
MLP_W = MLP_GROUPS * MLP_GROUP_DIM
GROUP_W = HEADS_PER_GROUP * HEAD_DIM
REST_W = SSD_W + 3 * MLP_W
MAIN_W = XBC_W + REST_W
INPROJ_STEPS = 8
CONV_SLAB = XBC_W // INPROJ_STEPS
PLAIN_SLAB = REST_W // INPROJ_STEPS
LANES = 128
SCAN_CHUNKS = 4
LATENT_TM_IN = 1024
LATENT_TM_OUT = 512

VMEM_LIMIT = 56 * 1024 * 1024


def _silu(v):
    h = 0.5 * v
    return h * (1.0 + jnp.tanh(h))


def _rms(v):
    return v * lax.rsqrt(jnp.mean(v * v, axis=-1, keepdims=True) + EPS)


def _dot_nt(a, b):
    return lax.dot_general(a, b, (((1,), (1,)), ((), ())), preferred_element_type=F32)


def _split3(v):
    v1 = v.astype(BF16)
    r1 = v - v1.astype(F32)
    v2 = r1.astype(BF16)
    r2 = r1 - v2.astype(F32)
    return v1, v2, r2.astype(BF16)


def _mod_kernel(cc_ref, w_ref, b_ref, o_ref):
    sc = _silu(cc_ref[...])
    o_ref[0] = jnp.dot(sc, w_ref[0], preferred_element_type=F32) + b_ref[0]


def _modulation(cc, w_ada, b_ada):
    depth, d, d3 = w_ada.shape
    tn = 768
    return pl.pallas_call(
        _mod_kernel,
        grid=(depth, d3 // tn),
        in_specs=[
            pl.BlockSpec((8, d), lambda l, j: (0, 0)),
            pl.BlockSpec((1, d, tn), lambda l, j: (l, 0, j)),
            pl.BlockSpec((1, 1, tn), lambda l, j: (l, 0, j)),
        ],
        out_specs=pl.BlockSpec((1, 8, tn), lambda l, j: (l, 0, j)),
        out_shape=jax.ShapeDtypeStruct((depth, 8, d3), F32),
        compiler_params=pltpu.CompilerParams(
            dimension_semantics=("parallel", "parallel"), vmem_limit_bytes=VMEM_LIMIT),
        name="adaln_mod",
    )(cc, w_ada, b_ada.reshape(depth, 1, d3))


def _inproj_kernel(x_ref, gs_ref, sh_ref, wc_ref, wp_ref, wdt_ref, cw_ref, cb_ref,
                   zx_ref, zr_ref, dt_ref, hx_ref, *, row_len):
    j = pl.program_id(1)
    tm, tc = zx_ref.shape

    @pl.when(j == 0)
    def _():
        hb = (_rms(x_ref[...]) * gs_ref[0] + sh_ref[0]).astype(BF16)
        hx_ref[...] = hb
        dt_ref[...] = _dot_nt(hb, wdt_ref[...])

    hx = hx_ref[...]
    acc = _dot_nt(hx, wc_ref[...])
    zr_ref[...] = _dot_nt(hx, wp_ref[...]).astype(BF16)

    rows3 = acc.reshape(tm // row_len, row_len, tc)
    y = cb_ref[...] + cw_ref[CONV_PAD:CONV_PAD + 1, :] * acc
    for d in (-2, -1, 1, 2):
        pad = jnp.zeros((tm // row_len, abs(d), tc), F32)
        if d < 0:
            shifted = jnp.concatenate([pad, rows3[:, :row_len + d, :]], axis=1)
        else:
            shifted = jnp.concatenate([rows3[:, d:, :], pad], axis=1)
        y = y + cw_ref[CONV_PAD + d:CONV_PAD + d + 1, :] * shifted.reshape(tm, tc)
    zx_ref[...] = _silu(y).astype(BF16)


def _inproj(x2, gs, shift, wts, layer, *, seq_len, row_len, tm):
    tokens, d = x2.shape
    assert gs.shape[0] == 1 or seq_len % tm == 0
    mod_row = (lambda i: 0) if gs.shape[0] == 1 else (lambda i: (i * tm) // seq_len)
    kern = functools.partial(_inproj_kernel, row_len=row_len)
    return pl.pallas_call(
        kern,
        grid=(tokens // tm, INPROJ_STEPS),
        in_specs=[
            pl.BlockSpec((tm, d), lambda i, j: (i, 0)),
            pl.BlockSpec((1, 1, d), lambda i, j: (mod_row(i), 0, 0)),
            pl.BlockSpec((1, 1, d), lambda i, j: (mod_row(i), 0, 0)),
            pl.BlockSpec((None, CONV_SLAB, d), lambda i, j: (layer, j, 0)),
            pl.BlockSpec((pl.Element(PLAIN_SLAB), pl.Element(d)),
                         lambda i, j: (pl.multiple_of(
                             layer * (MAIN_W + DT_W) + XBC_W + DT_W + j * PLAIN_SLAB, DT_W), 0)),
            pl.BlockSpec((None, LANES, d), lambda i, j: (layer, XBC_W // LANES, 0)),
            pl.BlockSpec((None, 8, CONV_SLAB), lambda i, j: (layer, 0, j)),
            pl.BlockSpec((None, 1, CONV_SLAB), lambda i, j: (layer, 0, j)),
        ],
        out_specs=[
            pl.BlockSpec((tm, CONV_SLAB), lambda i, j: (i, j)),
            pl.BlockSpec((tm, PLAIN_SLAB), lambda i, j: (i, j)),
            pl.BlockSpec((tm, LANES), lambda i, j: (i, 0)),
        ],
        out_shape=[
            jax.ShapeDtypeStruct((tokens, XBC_W), BF16),
            jax.ShapeDtypeStruct((tokens, REST_W), BF16),
            jax.ShapeDtypeStruct((tokens, LANES), F32),
        ],
        scratch_shapes=[pltpu.VMEM((tm, d), BF16)],
        compiler_params=pltpu.CompilerParams(
            dimension_semantics=("parallel", "arbitrary"), vmem_limit_bytes=VMEM_LIMIT),
        name="inproj",
    )(x2, gs, shift, wts["w_conv"], wts["w_plain"], wts["w_dt"], wts["cw8"], wts["cb"])


def _chunk_prep(dtr, bias, a_neg):
    q = dtr.shape[0]
    xx = dtr + bias
    dt = jnp.maximum(xx, 0.0) + jnp.log1p(jnp.exp(-jnp.abs(xx)))
    a = dt * a_neg * LOG2E
    acat = jnp.concatenate(_split3(a), axis=0)
    row = lax.broadcasted_iota(jnp.int32, (q, q), 0)
    col = lax.broadcasted_iota(jnp.int32, (q, q), 1)
    t_inc = jnp.where(col <= row, 1.0, 0.0).astype(BF16)
    t_suf = jnp.where(col >= row, 1.0, 0.0).astype(BF16)
    incl = jnp.dot(jnp.concatenate([t_inc] * 3, axis=1), acat, preferred_element_type=F32)
    suf = jnp.dot(jnp.concatenate([t_suf] * 3, axis=1), acat, preferred_element_type=F32)
    lane = lax.broadcasted_iota(jnp.int32, (q, LANES), 1)
    fwd = lane < HEADS
    cum = jnp.where(fwd, incl, suf)
    tot = jnp.where(fwd, incl[q - 1:q, :], suf[0:1, :])
    w = dt * jnp.exp2(tot - cum)
    lo = lane < DT_W
    dt_sum = dt + pltpu.roll(dt, LANES - HEADS, axis=1)
    rowp = jnp.where(lo, cum - jnp.log2(dt), pltpu.roll(jnp.log2(dt_sum), DT_W, axis=1))
    vals = jnp.where(lo, jnp.exp2(cum), pltpu.roll(w, DT_W, axis=1))
    return cum, rowp, vals


def _expand(vals, wanted, e_ref):
    v1 = vals.astype(BF16).astype(F32)
    r1 = vals - v1
    v2 = r1.astype(BF16).astype(F32)
    terms = (v1, v2, r1 - v2)
    packed = []
    for which, rows in wanted:
        n = rows.stop - rows.start
        lane = lax.broadcasted_iota(jnp.int32, (n, LANES), 1)
        p = jnp.zeros((n, LANES), F32)
        for k in (2, 1, 0):
            t = terms[k][rows]
            shift = (HEADS * (k - which)) % LANES
            t = t if shift == 0 else pltpu.roll(t, shift, axis=1)
            p = jnp.where(lane < HEADS * (k + 1), t, p)
        packed.append(p.astype(BF16))
    stacked = packed[0] if len(packed) == 1 else jnp.concatenate(packed, axis=0)
    return jnp.dot(stacked, e_ref[...], preferred_element_type=F32)


def _state_outer(b_g, xw_g):
    return lax.dot_general(b_g, xw_g, (((0,), (0,)), ((), ())), preferred_element_type=F32)


def _bwd_kernel(x_ref, b_ref, dt_ref, bias_ref, a_ref, e_ref, h0_ref,
                hs_ref, hfin_ref, cum_ref, rowp_ref, vals_ref, st_ref, *, ck):
    s = pl.program_id(1)

    @pl.when(s == 0)
    def _():
        st_ref[...] = h0_ref[0]

    q = CHUNK
    vals_all = []
    for ci in range(ck):
        tok = slice(ci * q, (ci + 1) * q)
        cum, rowp, vals = _chunk_prep(dt_ref[tok, :], bias_ref[...], a_ref[...])
        cum_ref[tok, :] = cum
        rowp_ref[tok, :] = rowp
        vals_ref[tok, :] = vals
        vals_all.append(vals)
    wanted = []
    for ci in range(ck):
        wanted += [(3, slice(ci * q, (ci + 1) * q)), (1, slice(ci * q, ci * q + 16))]
    ex_all = _expand(jnp.concatenate(vals_all, axis=0), wanted, e_ref)
    per_chunk = q + 16

    for ci in reversed(range(ck)):
        tok = slice(ci * q, (ci + 1) * q)
        base = ci * per_chunk
        hs_ref[0, ci] = st_ref[...].astype(BF16)
        for g in range(GROUPS):
            rows = slice(g * STATE, (g + 1) * STATE)
            cols = slice(g * GROUP_W, (g + 1) * GROUP_W)
            xw = (x_ref[tok, cols].astype(F32) * ex_all[base:base + q, cols]).astype(BF16)
            dec = ex_all[base + q:base + q + 1, cols]
            st_ref[rows, :] = st_ref[rows, :] * dec + _state_outer(b_ref[tok, rows], xw)

    @pl.when(s == pl.num_programs(1) - 1)
    def _():
        hfin_ref[0] = st_ref[...]


def _bwd_scan(zx, dt, bias, a_neg, e_mat, h0, *, n_seq, seq_len, ck):
    ns = seq_len // (ck * CHUNK)
    nc = seq_len // CHUNK
    tm = ck * CHUNK
    tok = lambda b, s: b * ns + (ns - 1 - s)
    return pl.pallas_call(
        functools.partial(_bwd_kernel, ck=ck),
        grid=(n_seq, ns),
        in_specs=[
            pl.BlockSpec((tm, SSD_W), lambda b, s: (tok(b, s), 0)),
            pl.BlockSpec((tm, GN), lambda b, s: (tok(b, s), SSD_W // GN)),
            pl.BlockSpec((tm, LANES), lambda b, s: (tok(b, s), 0)),
            pl.BlockSpec((1, LANES), lambda b, s: (0, 0)),
            pl.BlockSpec((1, LANES), lambda b, s: (0, 0)),
            pl.BlockSpec((LANES, SSD_W), lambda b, s: (0, 0)),
            pl.BlockSpec((1, GN, GROUP_W), lambda b, s: (b, 0, 0)),
        ],
        out_specs=[
            pl.BlockSpec((1, ck, GN, GROUP_W), lambda b, s: (b, ns - 1 - s, 0, 0)),
            pl.BlockSpec((1, GN, GROUP_W), lambda b, s: (b, 0, 0)),
            pl.BlockSpec((tm, LANES), lambda b, s: (tok(b, s), 0)),
            pl.BlockSpec((tm, LANES), lambda b, s: (tok(b, s), 0)),
            pl.BlockSpec((tm, LANES), lambda b, s: (tok(b, s), 0)),
        ],
        out_shape=[
            jax.ShapeDtypeStruct((n_seq, nc, GN, GROUP_W), BF16),
            jax.ShapeDtypeStruct((n_seq, GN, GROUP_W), F32),
            jax.ShapeDtypeStruct((n_seq * seq_len, LANES), F32),
            jax.ShapeDtypeStruct((n_seq * seq_len, LANES), F32),
            jax.ShapeDtypeStruct((n_seq * seq_len, LANES), F32),
        ],
        scratch_shapes=[pltpu.VMEM((GN, GROUP_W), F32)],
        compiler_params=pltpu.CompilerParams(
            dimension_semantics=("parallel", "arbitrary"), vmem_limit_bytes=VMEM_LIMIT),
        name="ssd_bwd_scan",
    )(zx, zx, dt, bias, a_neg, e_mat, h0)


def _ssd_kernel(x_ref, b_ref, c_ref, zs_ref, cum_ref, rowp_ref, vals_ref, e_ref, h0_ref, hb_ref,
                dskip_ref, ya_ref, hfin_ref, st_ref, y_ref, *, ck):
    s = pl.program_id(1)
    q = CHUNK

    @pl.when(s == 0)
    def _():
        st_ref[...] = h0_ref[0]

    row = lax.broadcasted_iota(jnp.int32, (q, q), 0)
    col = lax.broadcasted_iota(jnp.int32, (q, q), 1)
    lower = col < row
    upper = col > row
    first_head = lax.broadcasted_iota(jnp.int32, (q, 2 * HEAD_DIM), 1) < HEAD_DIM

    for ci in range(ck):
        tok = slice(ci * q, (ci + 1) * q)
        cum, vals = cum_ref[tok, :], vals_ref[tok, :]
        row_t = rowp_ref[tok, :].T
        ex = _expand(vals, [(0, slice(0, q)), (1, slice(0, q)), (2, slice(0, q))], e_ref)

        for g in range(GROUPS):
            rows = slice(g * STATE, (g + 1) * STATE)
            cols = slice(g * GROUP_W, (g + 1) * GROUP_W)
            x_g = x_ref[tok, cols]
            xf_g = x_g.astype(F32)
            b_g = b_ref[tok, rows]
            c_g = c_ref[tok, rows]
            cb = lax.dot_general(c_g, b_g, (((1,), (1,)), ((), ())), preferred_element_type=F32)
            ms = []
            for r in range(HEADS_PER_GROUP):
                h = g * HEADS_PER_GROUP + r
                cum_f = jnp.broadcast_to(cum[:, h:h + 1], (q, q))
                cum_b = jnp.broadcast_to(cum[:, HEADS + h:HEADS + h + 1], (q, q))
                arg = jnp.where(lower, cum_f - row_t[h:h + 1, :],
                                jnp.where(upper, cum_b - row_t[HEADS + h:HEADS + h + 1, :],
                                          row_t[2 * HEADS + h:2 * HEADS + h + 1, :]))
                ms.append((cb * jnp.exp2(arg)).astype(BF16))
            ys = []
            for pr in range(HEADS_PER_GROUP // 2):
                xp = x_g[:, pr * 2 * HEAD_DIM:(pr + 1) * 2 * HEAD_DIM]
                zero = jnp.zeros_like(xp)
                rhs = jnp.concatenate([jnp.where(first_head, xp, zero), jnp.where(first_head, zero, xp)], axis=0)
                lhs = jnp.concatenate([ms[2 * pr], ms[2 * pr + 1]], axis=1)
                ys.append(jnp.dot(lhs, rhs, preferred_element_type=F32))
            y_diag = jnp.concatenate(ys, axis=1)
            e_f = ex[0:q, cols]
            y_off = (jnp.dot(c_g, st_ref[rows, :].astype(BF16), preferred_element_type=F32) * e_f
                     + jnp.dot(c_g, hb_ref[0, ci, rows, :], preferred_element_type=F32) * ex[q:2 * q, cols])
            y_ref[tok, cols] = y_diag + y_off + dskip_ref[:, cols] * xf_g
            xw = (xf_g * ex[2 * q:3 * q, cols]).astype(BF16)
            st_ref[rows, :] = st_ref[rows, :] * e_f[q - 1:q, :] + _state_outer(b_g, xw)

        gated = y_ref[tok, :] * _silu(zs_ref[tok, :].astype(F32))
        ya_ref[tok, :] = _rms(gated).astype(BF16)

    @pl.when(s == pl.num_programs(1) - 1)
    def _():
        hfin_ref[0] = st_ref[...]


def _ssd(zx, zr, cum, rowp, vals, e_mat, h0f, hb_starts, dskip, *, n_seq, seq_len, ck):
    ns = seq_len // (ck * CHUNK)
    tm = ck * CHUNK
    tok = lambda b, s: b * ns + s
    tokens = n_seq * seq_len
    return pl.pallas_call(
        functools.partial(_ssd_kernel, ck=ck),
        grid=(n_seq, ns),
        in_specs=[
            pl.BlockSpec((tm, SSD_W), lambda b, s: (tok(b, s), 0)),
            pl.BlockSpec((tm, GN), lambda b, s: (tok(b, s), SSD_W // GN)),
            pl.BlockSpec((tm, GN), lambda b, s: (tok(b, s), SSD_W // GN + 1)),
            pl.BlockSpec((tm, SSD_W), lambda b, s: (tok(b, s), 0)),
            pl.BlockSpec((tm, LANES), lambda b, s: (tok(b, s), 0)),
            pl.BlockSpec((tm, LANES), lambda b, s: (tok(b, s), 0)),
            pl.BlockSpec((tm, LANES), lambda b, s: (tok(b, s), 0)),
            pl.BlockSpec((LANES, SSD_W), lambda b, s: (0, 0)),
            pl.BlockSpec((1, GN, GROUP_W), lambda b, s: (b, 0, 0)),
            pl.BlockSpec((1, ck, GN, GROUP_W), lambda b, s: (b, s, 0, 0)),
            pl.BlockSpec((1, SSD_W), lambda b, s: (0, 0)),
        ],
        out_specs=[
            pl.BlockSpec((tm, SSD_W), lambda b, s: (tok(b, s), 0)),
            pl.BlockSpec((1, GN, GROUP_W), lambda b, s: (b, 0, 0)),
        ],
        out_shape=[
            jax.ShapeDtypeStruct((tokens, SSD_W), BF16),
            jax.ShapeDtypeStruct((n_seq, GN, GROUP_W), F32),
        ],
        scratch_shapes=[pltpu.VMEM((GN, GROUP_W), F32), pltpu.VMEM((tm, SSD_W), F32)],
        compiler_params=pltpu.CompilerParams(
            dimension_semantics=("parallel", "arbitrary"), vmem_limit_bytes=VMEM_LIMIT),
        name="ssd_main",
    )(zx, zx, zx, zr, cum, rowp, vals, e_mat, h0f, hb_starts, dskip)


def _out_kernel(ya_ref, u_ref, v_ref, zm_ref, x_ref, gate_ref, gv_ref,
                ws_ref, bs_ref, wout_ref, o_ref, sg_ref):
    tm = x_ref.shape[0]
    vn = (_rms(v_ref[...].astype(F32)) * gv_ref[...]).astype(BF16)
    for c in range(tm // MLP_CHUNK):
        rows = slice(c * MLP_CHUNK, (c + 1) * MLP_CHUNK)
        for g in range(MLP_GROUPS):
            cols = slice(g * MLP_GROUP_DIM, (g + 1) * MLP_GROUP_DIM)
            sg_ref[rows, cols] = jnp.dot(ws_ref[g], vn[rows, cols], preferred_element_type=F32) + bs_ref[:, cols]
    yb = _rms(u_ref[...].astype(F32) * sg_ref[...] * _silu(zm_ref[...].astype(F32))).astype(BF16)
    mixed = (jnp.dot(ya_ref[...], wout_ref[0:SSD_W, :], preferred_element_type=F32)
             + jnp.dot(yb, wout_ref[SSD_W:, :], preferred_element_type=F32))
    o_ref[...] = x_ref[...] + gate_ref[0] * _rms(mixed)


def _out(ya, zr, x2, gate, g_v, bs_mat, wts, layer, *, seq_len, tm):
    tokens, d = x2.shape
    assert gate.shape[0] == 1 or seq_len % tm == 0
    mod_row = (lambda i: 0) if gate.shape[0] == 1 else (lambda i: (i * tm) // seq_len)
    col0 = SSD_W // MLP_W
    const2 = lambda i: (0, 0)
    return pl.pallas_call(
        _out_kernel,
        grid=(tokens // tm,),
        in_specs=[
            pl.BlockSpec((tm, SSD_W), lambda i: (i, 0)),
            pl.BlockSpec((tm, MLP_W), lambda i: (i, col0)),
            pl.BlockSpec((tm, MLP_W), lambda i: (i, col0 + 1)),
            pl.BlockSpec((tm, MLP_W), lambda i: (i, col0 + 2)),
            pl.BlockSpec((tm, d), lambda i: (i, 0)),
            pl.BlockSpec((1, 1, d), lambda i: (mod_row(i), 0, 0)),
            pl.BlockSpec((1, MLP_W), const2),
            pl.BlockSpec((None, MLP_GROUPS, MLP_CHUNK, MLP_CHUNK), lambda i: (layer, 0, 0, 0)),
            pl.BlockSpec((MLP_CHUNK, MLP_W), const2),
            pl.BlockSpec((None, SSD_W + MLP_W, d), lambda i: (layer, 0, 0), pipeline_mode=pl.Buffered(1)),
        ],
        out_specs=pl.BlockSpec((tm, d), lambda i: (i, 0)),
        out_shape=jax.ShapeDtypeStruct((tokens, d), F32),
        scratch_shapes=[pltpu.VMEM((tm, MLP_W), F32)],
        compiler_params=pltpu.CompilerParams(
            dimension_semantics=("parallel",), vmem_limit_bytes=VMEM_LIMIT),
        name="mix_out",
    )(ya, zr, zr, zr, x2, gate, g_v, wts["w_s"], bs_mat, wts["w_out"])


def _expand_matrix():
    j = np.arange(LANES)[:, None]
    h = np.arange(SSD_W)[None, :] // HEAD_DIM
    return jnp.asarray((j % HEADS == h) & (j < 3 * HEADS), dtype=BF16)


def _stream_layer(x2, wts, p, layer, mod_rows, h0f, h0b, *, n_seq, seq_len, row_len, tm_in, tm_out):
    d = x2.shape[1]
    shift = mod_rows[:, None, 0:d]
    gs = p["g_pre"] * (1.0 + mod_rows[:, None, d:2 * d])
    gate = mod_rows[:, None, 2 * d:3 * d] * p["g_post"]
    zx, zr, dt = _inproj(x2, gs, shift, wts, layer, seq_len=seq_len, row_len=row_len, tm=tm_in)
    ck = min(SCAN_CHUNKS, seq_len // CHUNK)
    hb_starts, hb_fin, cum, rowp, vals = _bwd_scan(zx, dt, p["bias"], p["a_neg"], p["e_mat"], h0b,
                                                   n_seq=n_seq, seq_len=seq_len, ck=ck)
    ya, hf_fin = _ssd(zx, zr, cum, rowp, vals, p["e_mat"], h0f, hb_starts,
                      p["dskip"], n_seq=n_seq, seq_len=seq_len, ck=ck)
    x_new = _out(ya, zr, x2, gate, p["g_v"], p["bs_mat"], wts, layer,
                 seq_len=seq_len, tm=tm_out)
    return x_new, hf_fin, hb_fin


def _all_layer_weights(w_in, conv_w, conv_b, g_ssd, g_mlp, w_s, w_out):
    out_gain = jnp.concatenate([g_ssd, g_mlp], axis=1)[:, :, None]
    w16 = lax.optimization_barrier(jnp.swapaxes(w_in, 1, 2).astype(BF16))
    return {
        "w_conv": w16,
        "w_plain": w16.reshape(-1, w16.shape[2]),
        "w_dt": w16,
        "cw8": jnp.pad(conv_w, ((0, 0), (0, 8 - CONV_W), (0, 0))),
        "cb": conv_b[:, None, :],
        "w_s": w_s.astype(BF16),
        "w_out": (w_out * out_gain).astype(BF16),
    }


def _layer_params(l, g_pre, g_post, dt_bias, a_log, d_skip, g_v, b_s, e_mat):
    pad = jnp.zeros((LANES - DT_W,), F32)
    return {
        "g_pre": g_pre[l][None, None, :],
        "g_post": g_post[l][None, :],
        "bias": jnp.concatenate([dt_bias[l].reshape(-1), pad])[None, :],
        "a_neg": jnp.concatenate([-jnp.exp(a_log[l].reshape(-1)), pad])[None, :],
        "dskip": jnp.repeat(d_skip[l], HEAD_DIM)[None, :],
        "g_v": g_v[l][None, :],
        "bs_mat": jnp.repeat(b_s[l].T, MLP_GROUP_DIM, axis=1),
        "e_mat": e_mat,
    }


def kernel(x, c, ctx, c_ctx, w_ada, b_ada, g_pre, g_post, w_in, conv_w, conv_b, dt_bias, a_log,
           d_skip, g_ssd, g_v, w_s, b_s, g_mlp, w_out):
    bsz, seq, d = x.shape
    ctx_len = ctx.shape[1]
    depth = w_in.shape[0]
    assert seq % LATENT_TM_IN == 0 and ctx_len % CHUNK == 0 and ctx_len & (ctx_len - 1) == 0
    assert w_in.shape[2] == MAIN_W + DT_W and bsz + 1 <= 8

    cc = jnp.concatenate([c, c_ctx[None, :], jnp.zeros((8 - bsz - 1, d), F32)], axis=0)
    mod = _modulation(cc, w_ada, b_ada)
    e_mat = _expand_matrix()
    wts = _all_layer_weights(w_in, conv_w, conv_b, g_ssd, g_mlp, w_s, w_out)

    xs = x.reshape(bsz * seq, d)
    cs = ctx.reshape(bsz * ctx_len, d)
    zeros_h = jnp.zeros((bsz, GN, GROUP_W), F32)
    for l in range(depth):
        p = _layer_params(l, g_pre, g_post, dt_bias, a_log, d_skip, g_v, b_s, e_mat)
        cs, h_f, h_b = _stream_layer(cs, wts, p, l, mod[l, bsz:bsz + 1], zeros_h, zeros_h, n_seq=bsz,
                                     seq_len=ctx_len, row_len=ctx_len, tm_in=bsz * ctx_len, tm_out=ctx_len)
        xs, _, _ = _stream_layer(xs, wts, p, l, mod[l, :bsz], h_f, h_b, n_seq=bsz, seq_len=seq,
                                 row_len=GRID_W, tm_in=LATENT_TM_IN, tm_out=LATENT_TM_OUT)
    return xs.reshape(bsz, seq, d)
```

```python
import functools

import jax
import jax.numpy as jnp
import numpy as np
from jax import lax
from jax.experimental import pallas as pl
from jax.experimental.pallas import tpu as pltpu

F32 = jnp.float32
BF16 = jnp.bfloat16

GRID_W = 64
HEADS = 32
HEAD_DIM = 64
GROUPS = 8
HEADS_PER_GROUP = HEADS // GROUPS
STATE = 128
CHUNK = 128
CONV_W = 5
CONV_PAD = CONV_W // 2
MLP_GROUPS = 16
MLP_GROUP_DIM = 128
MLP_CHUNK = 128
EPS = 1e-6
LOG2E = 1.4426950408889634

SSD_W = HEADS * HEAD_DIM
GN = GROUPS * STATE
XBC_W = SSD_W + 2 * GN
DT_W = 2 * HEADS
MLP_W = MLP_GROUPS * MLP_GROUP_DIM
GROUP_W = HEADS_PER_GROUP * HEAD_DIM
REST_W = SSD_W + 3 * MLP_W
MAIN_W = XBC_W + REST_W
INPROJ_STEPS = 8
CONV_SLAB = XBC_W // INPROJ_STEPS
PLAIN_SLAB = REST_W // INPROJ_STEPS
LANES = 128
SCAN_CHUNKS = 4
LATENT_TM_IN = 1024
LATENT_TM_OUT = 512

VMEM_LIMIT = 56 * 1024 * 1024


def _silu(v):
    h = 0.5 * v
    return h * (1.0 + jnp.tanh(h))


def _rms(v):
    return v * lax.rsqrt(jnp.mean(v * v, axis=-1, keepdims=True) + EPS)


def _dot_nt(a, b):
    return lax.dot_general(a, b, (((1,), (1,)), ((), ())), preferred_element_type=F32)


def _split3(v):
    v1 = v.astype(BF16)
    r1 = v - v1.astype(F32)
    v2 = r1.astype(BF16)
    r2 = r1 - v2.astype(F32)
    return v1, v2, r2.astype(BF16)


def _mod_kernel(cc_ref, w_ref, b_ref, o_ref):
    sc = _silu(cc_ref[...])
    o_ref[0] = jnp.dot(sc, w_ref[0], preferred_element_type=F32) + b_ref[0]


def _modulation(cc, w_ada, b_ada):
    depth, d, d3 = w_ada.shape
    tn = 768
    return pl.pallas_call(
        _mod_kernel,
        grid=(depth, d3 // tn),
        in_specs=[
            pl.BlockSpec((8, d), lambda l, j: (0, 0)),
            pl.BlockSpec((1, d, tn), lambda l, j: (l, 0, j)),
            pl.BlockSpec((1, 1, tn), lambda l, j: (l, 0, j)),
        ],
        out_specs=pl.BlockSpec((1, 8, tn), lambda l, j: (l, 0, j)),
        out_shape=jax.ShapeDtypeStruct((depth, 8, d3), F32),
        compiler_params=pltpu.CompilerParams(
            dimension_semantics=("parallel", "parallel"), vmem_limit_bytes=VMEM_LIMIT),
        name="adaln_mod",
    )(cc, w_ada, b_ada.reshape(depth, 1, d3))


def _inproj_kernel(x_ref, gs_ref, sh_ref, wc_ref, wp_ref, wdt_ref, cw_ref, cb_ref,
                   zx_ref, zr_ref, dt_ref, hx_ref, *, row_len):
    j = pl.program_id(1)
    tm, tc = zx_ref.shape

    @pl.when(j == 0)
    def _():
        hb = (_rms(x_ref[...]) * gs_ref[0] + sh_ref[0]).astype(BF16)
        hx_ref[...] = hb
        dt_ref[...] = _dot_nt(hb, wdt_ref[...])

    hx = hx_ref[...]
    acc = _dot_nt(hx, wc_ref[...])
    zr_ref[...] = _dot_nt(hx, wp_ref[...]).astype(BF16)

    rows3 = acc.reshape(tm // row_len, row_len, tc)
    y = cb_ref[...] + cw_ref[CONV_PAD:CONV_PAD + 1, :] * acc
    for d in (-2, -1, 1, 2):
        pad = jnp.zeros((tm // row_len, abs(d), tc), F32)
        if d < 0:
            shifted = jnp.concatenate([pad, rows3[:, :row_len + d, :]], axis=1)
        else:
            shifted = jnp.concatenate([rows3[:, d:, :], pad], axis=1)
        y = y + cw_ref[CONV_PAD + d:CONV_PAD + d + 1, :] * shifted.reshape(tm, tc)
    zx_ref[...] = _silu(y).astype(BF16)


def _inproj(x2, gs, shift, wts, layer, *, seq_len, row_len, tm):
    tokens, d = x2.shape
    assert gs.shape[0] == 1 or seq_len % tm == 0
    mod_row = (lambda i: 0) if gs.shape[0] == 1 else (lambda i: (i * tm) // seq_len)
    kern = functools.partial(_inproj_kernel, row_len=row_len)
    return pl.pallas_call(
        kern,
        grid=(tokens // tm, INPROJ_STEPS),
        in_specs=[
            pl.BlockSpec((tm, d), lambda i, j: (i, 0)),
            pl.BlockSpec((1, 1, d), lambda i, j: (mod_row(i), 0, 0)),
            pl.BlockSpec((1, 1, d), lambda i, j: (mod_row(i), 0, 0)),
            pl.BlockSpec((None, CONV_SLAB, d), lambda i, j: (layer, j, 0)),
            pl.BlockSpec((pl.Element(PLAIN_SLAB), pl.Element(d)),
                         lambda i, j: (pl.multiple_of(
                             layer * (MAIN_W + DT_W) + XBC_W + DT_W + j * PLAIN_SLAB, DT_W), 0)),
            pl.BlockSpec((None, LANES, d), lambda i, j: (layer, XBC_W // LANES, 0)),
            pl.BlockSpec((None, 8, CONV_SLAB), lambda i, j: (layer, 0, j)),
            pl.BlockSpec((None, 1, CONV_SLAB), lambda i, j: (layer, 0, j)),
        ],
        out_specs=[
            pl.BlockSpec((tm, CONV_SLAB), lambda i, j: (i, j)),
            pl.BlockSpec((tm, PLAIN_SLAB), lambda i, j: (i, j)),
            pl.BlockSpec((tm, LANES), lambda i, j: (i, 0)),
        ],
        out_shape=[
            jax.ShapeDtypeStruct((tokens, XBC_W), BF16),
            jax.ShapeDtypeStruct((tokens, REST_W), BF16),
            jax.ShapeDtypeStruct((tokens, LANES), F32),
        ],
        scratch_shapes=[pltpu.VMEM((tm, d), BF16)],
        compiler_params=pltpu.CompilerParams(
            dimension_semantics=("parallel", "arbitrary"), vmem_limit_bytes=VMEM_LIMIT),
        name="inproj",
    )(x2, gs, shift, wts["w_conv"], wts["w_plain"], wts["w_dt"], wts["cw8"], wts["cb"])


def _chunk_prep(dtr, bias, a_neg):
    q = dtr.shape[0]
    xx = dtr + bias
    dt = jnp.maximum(xx, 0.0) + jnp.log1p(jnp.exp(-jnp.abs(xx)))
    a = dt * a_neg * LOG2E
    acat = jnp.concatenate(_split3(a), axis=0)
    row = lax.broadcasted_iota(jnp.int32, (q, q), 0)
    col = lax.broadcasted_iota(jnp.int32, (q, q), 1)
    t_inc = jnp.where(col <= row, 1.0, 0.0).astype(BF16)
    t_suf = jnp.where(col >= row, 1.0, 0.0).astype(BF16)
    incl = jnp.dot(jnp.concatenate([t_inc] * 3, axis=1), acat, preferred_element_type=F32)
    suf = jnp.dot(jnp.concatenate([t_suf] * 3, axis=1), acat, preferred_element_type=F32)
    lane = lax.broadcasted_iota(jnp.int32, (q, LANES), 1)
    fwd = lane < HEADS
    cum = jnp.where(fwd, incl, suf)
    tot = jnp.where(fwd, incl[q - 1:q, :], suf[0:1, :])
    w = dt * jnp.exp2(tot - cum)
    lo = lane < DT_W
    dt_sum = dt + pltpu.roll(dt, LANES - HEADS, axis=1)
    rowp = jnp.where(lo, cum - jnp.log2(dt), pltpu.roll(-jnp.log2(dt_sum), DT_W, axis=1))
    vals = jnp.where(lo, jnp.exp2(cum), pltpu.roll(w, DT_W, axis=1))
    return cum, rowp, vals


def _expand(vals, wanted, e_ref):
    v1 = vals.astype(BF16).astype(F32)
    r1 = vals - v1
    v2 = r1.astype(BF16).astype(F32)
    terms = (v1, v2, r1 - v2)
    packed = []
    for which, rows in wanted:
        n = rows.stop - rows.start
        lane = lax.broadcasted_iota(jnp.int32, (n, LANES), 1)
        p = jnp.zeros((n, LANES), F32)
        for k in (2, 1, 0):
            t = terms[k][rows]
            shift = (HEADS * (k - which)) % LANES
            t = t if shift == 0 else pltpu.roll(t, shift, axis=1)
            p = jnp.where(lane < HEADS * (k + 1), t, p)
        packed.append(p.astype(BF16))
    stacked = packed[0] if len(packed) == 1 else jnp.concatenate(packed, axis=0)
    return jnp.dot(stacked, e_ref[...], preferred_element_type=F32)


def _state_outer(b_g, xw_g):
    return lax.dot_general(b_g, xw_g, (((0,), (0,)), ((), ())), preferred_element_type=F32)


def _bwd_kernel(x_ref, b_ref, dt_ref, bias_ref, a_ref, e_ref, h0_ref,
                hs_ref, hfin_ref, cum_ref, rowp_ref, vals_ref, st_ref, *, ck):
    s = pl.program_id(1)

    @pl.when(s == 0)
    def _():
        st_ref[...] = h0_ref[0]

    q = CHUNK
    vals_all = []
    for ci in range(ck):
        tok = slice(ci * q, (ci + 1) * q)
        cum, rowp, vals = _chunk_prep(dt_ref[tok, :], bias_ref[...], a_ref[...])
        cum_ref[tok, :] = cum
        rowp_ref[tok, :] = rowp
        vals_ref[tok, :] = vals
        vals_all.append(vals)
    wanted = []
    for ci in range(ck):
        wanted += [(3, slice(ci * q, (ci + 1) * q)), (1, slice(ci * q, ci * q + 16))]
    ex_all = _expand(jnp.concatenate(vals_all, axis=0), wanted, e_ref)
    per_chunk = q + 16

    for ci in reversed(range(ck)):
        tok = slice(ci * q, (ci + 1) * q)
        base = ci * per_chunk
        hs_ref[0, ci] = st_ref[...].astype(BF16)
        for g in range(GROUPS):
            rows = slice(g * STATE, (g + 1) * STATE)
            cols = slice(g * GROUP_W, (g + 1) * GROUP_W)
            xw = (x_ref[tok, cols].astype(F32) * ex_all[base:base + q, cols]).astype(BF16)
            dec = ex_all[base + q:base + q + 1, cols]
            st_ref[rows, :] = st_ref[rows, :] * dec + _state_outer(b_ref[tok, rows], xw)

    @pl.when(s == pl.num_programs(1) - 1)
    def _():
        hfin_ref[0] = st_ref[...]


def _bwd_scan(zx, dt, bias, a_neg, e_mat, h0, *, n_seq, seq_len, ck):
    ns = seq_len // (ck * CHUNK)
    nc = seq_len // CHUNK
    tm = ck * CHUNK
    tok = lambda b, s: b * ns + (ns - 1 - s)
    return pl.pallas_call(
        functools.partial(_bwd_kernel, ck=ck),
        grid=(n_seq, ns),
        in_specs=[
            pl.BlockSpec((tm, SSD_W), lambda b, s: (tok(b, s), 0)),
            pl.BlockSpec((tm, GN), lambda b, s: (tok(b, s), SSD_W // GN)),
            pl.BlockSpec((tm, LANES), lambda b, s: (tok(b, s), 0)),
            pl.BlockSpec((1, LANES), lambda b, s: (0, 0)),
            pl.BlockSpec((1, LANES), lambda b, s: (0, 0)),
            pl.BlockSpec((LANES, SSD_W), lambda b, s: (0, 0)),
            pl.BlockSpec((1, GN, GROUP_W), lambda b, s: (b, 0, 0)),
        ],
        out_specs=[
            pl.BlockSpec((1, ck, GN, GROUP_W), lambda b, s: (b, ns - 1 - s, 0, 0)),
            pl.BlockSpec((1, GN, GROUP_W), lambda b, s: (b, 0, 0)),
            pl.BlockSpec((tm, LANES), lambda b, s: (tok(b, s), 0)),
            pl.BlockSpec((tm, LANES), lambda b, s: (tok(b, s), 0)),
            pl.BlockSpec((tm, LANES), lambda b, s: (tok(b, s), 0)),
        ],
        out_shape=[
            jax.ShapeDtypeStruct((n_seq, nc, GN, GROUP_W), BF16),
            jax.ShapeDtypeStruct((n_seq, GN, GROUP_W), F32),
            jax.ShapeDtypeStruct((n_seq * seq_len, LANES), F32),
            jax.ShapeDtypeStruct((n_seq * seq_len, LANES), F32),
            jax.ShapeDtypeStruct((n_seq * seq_len, LANES), F32),
        ],
        scratch_shapes=[pltpu.VMEM((GN, GROUP_W), F32)],
        compiler_params=pltpu.CompilerParams(
            dimension_semantics=("parallel", "arbitrary"), vmem_limit_bytes=VMEM_LIMIT),
        name="ssd_bwd_scan",
    )(zx, zx, dt, bias, a_neg, e_mat, h0)


def _ssd_kernel(x_ref, b_ref, c_ref, zs_ref, cum_ref, rowp_ref, vals_ref, e_ref, h0_ref, hb_ref,
                dskip_ref, ya_ref, hfin_ref, st_ref, y_ref, *, ck):
    s = pl.program_id(1)
    q = CHUNK

    @pl.when(s == 0)
    def _():
        st_ref[...] = h0_ref[0]

    row = lax.broadcasted_iota(jnp.int32, (q, q), 0)
    col = lax.broadcasted_iota(jnp.int32, (q, q), 1)
    lower = col < row
    upper = col > row
    first_head = lax.broadcasted_iota(jnp.int32, (q, 2 * HEAD_DIM), 1) < HEAD_DIM
    lane_pick = jnp.where(lower, 0, jnp.where(upper, HEADS, 2 * HEADS)).astype(jnp.int32)

    for ci in range(ck):
        tok = slice(ci * q, (ci + 1) * q)
        cum, vals = cum_ref[tok, :], vals_ref[tok, :]
        row_t = rowp_ref[tok, :].T
        ex = _expand(vals, [(0, slice(0, q)), (1, slice(0, q)), (2, slice(0, q))], e_ref)

        for g in range(GROUPS):
            rows = slice(g * STATE, (g + 1) * STATE)
            cols = slice(g * GROUP_W, (g + 1) * GROUP_W)
            x_g = x_ref[tok, cols]
            xf_g = x_g.astype(F32)
            b_g = b_ref[tok, rows]
            c_g = c_ref[tok, rows]
            cb = lax.dot_general(c_g, b_g, (((1,), (1,)), ((), ())), preferred_element_type=F32)
            ms = []
            for r in range(HEADS_PER_GROUP):
                h = g * HEADS_PER_GROUP + r
                col_term = jnp.take_along_axis(cum, lane_pick + h, axis=1)
                row_term = jnp.where(lower, row_t[h:h + 1, :],
                                     jnp.where(upper, row_t[HEADS + h:HEADS + h + 1, :],
                                               row_t[2 * HEADS + h:2 * HEADS + h + 1, :]))
                ms.append((cb * jnp.exp2(col_term - row_term)).astype(BF16))
            ys = []
            for pr in range(HEADS_PER_GROUP // 2):
                xp = x_g[:, pr * 2 * HEAD_DIM:(pr + 1) * 2 * HEAD_DIM]
                zero = jnp.zeros_like(xp)
                rhs = jnp.concatenate([jnp.where(first_head, xp, zero), jnp.where(first_head, zero, xp)], axis=0)
                lhs = jnp.concatenate([ms[2 * pr], ms[2 * pr + 1]], axis=1)
                ys.append(jnp.dot(lhs, rhs, preferred_element_type=F32))
            y_diag = jnp.concatenate(ys, axis=1)
            e_f = ex[0:q, cols]
            y_off = (jnp.dot(c_g, st_ref[rows, :].astype(BF16), preferred_element_type=F32) * e_f
                     + jnp.dot(c_g, hb_ref[0, ci, rows, :], preferred_element_type=F32) * ex[q:2 * q, cols])
            y_ref[tok, cols] = y_diag + y_off + dskip_ref[:, cols] * xf_g
            xw = (xf_g * ex[2 * q:3 * q, cols]).astype(BF16)
            st_ref[rows, :] = st_ref[rows, :] * e_f[q - 1:q, :] + _state_outer(b_g, xw)

        gated = y_ref[tok, :] * _silu(zs_ref[tok, :].astype(F32))
        ya_ref[tok, :] = _rms(gated).astype(BF16)

    @pl.when(s == pl.num_programs(1) - 1)
    def _():
        hfin_ref[0] = st_ref[...]


def _ssd(zx, zr, cum, rowp, vals, e_mat, h0f, hb_starts, dskip, *, n_seq, seq_len, ck):
    ns = seq_len // (ck * CHUNK)
    tm = ck * CHUNK
    tok = lambda b, s: b * ns + s
    tokens = n_seq * seq_len
    return pl.pallas_call(
        functools.partial(_ssd_kernel, ck=ck),
        grid=(n_seq, ns),
        in_specs=[
            pl.BlockSpec((tm, SSD_W), lambda b, s: (tok(b, s), 0)),
            pl.BlockSpec((tm, GN), lambda b, s: (tok(b, s), SSD_W // GN)),
            pl.BlockSpec((tm, GN), lambda b, s: (tok(b, s), SSD_W // GN + 1)),
            pl.BlockSpec((tm, SSD_W), lambda b, s: (tok(b, s), 0)),
            pl.BlockSpec((tm, LANES), lambda b, s: (tok(b, s), 0)),
            pl.BlockSpec((tm, LANES), lambda b, s: (tok(b, s), 0)),
            pl.BlockSpec((tm, LANES), lambda b, s: (tok(b, s), 0)),
            pl.BlockSpec((LANES, SSD_W), lambda b, s: (0, 0)),
            pl.BlockSpec((1, GN, GROUP_W), lambda b, s: (b, 0, 0)),
            pl.BlockSpec((1, ck, GN, GROUP_W), lambda b, s: (b, s, 0, 0)),
            pl.BlockSpec((1, SSD_W), lambda b, s: (0, 0)),
        ],
        out_specs=[
            pl.BlockSpec((tm, SSD_W), lambda b, s: (tok(b, s), 0)),
            pl.BlockSpec((1, GN, GROUP_W), lambda b, s: (b, 0, 0)),
        ],
        out_shape=[
            jax.ShapeDtypeStruct((tokens, SSD_W), BF16),
            jax.ShapeDtypeStruct((n_seq, GN, GROUP_W), F32),
        ],
        scratch_shapes=[pltpu.VMEM((GN, GROUP_W), F32), pltpu.VMEM((tm, SSD_W), F32)],
        compiler_params=pltpu.CompilerParams(
            dimension_semantics=("parallel", "arbitrary"), vmem_limit_bytes=VMEM_LIMIT),
        name="ssd_main",
    )(zx, zx, zx, zr, cum, rowp, vals, e_mat, h0f, hb_starts, dskip)


def _out_kernel(ya_ref, u_ref, v_ref, zm_ref, x_ref, gate_ref, gv_ref,
                ws_ref, bs_ref, wout_ref, o_ref, sg_ref):
    tm = x_ref.shape[0]
    vn = (_rms(v_ref[...].astype(F32)) * gv_ref[...]).astype(BF16)
    for c in range(tm // MLP_CHUNK):
        rows = slice(c * MLP_CHUNK, (c + 1) * MLP_CHUNK)
        for g in range(MLP_GROUPS):
            cols = slice(g * MLP_GROUP_DIM, (g + 1) * MLP_GROUP_DIM)
            sg_ref[rows, cols] = jnp.dot(ws_ref[g], vn[rows, cols], preferred_element_type=F32) + bs_ref[:, cols]
    yb = _rms(u_ref[...].astype(F32) * sg_ref[...] * _silu(zm_ref[...].astype(F32))).astype(BF16)
    mixed = (jnp.dot(ya_ref[...], wout_ref[0:SSD_W, :], preferred_element_type=F32)
             + jnp.dot(yb, wout_ref[SSD_W:, :], preferred_element_type=F32))
    o_ref[...] = x_ref[...] + gate_ref[0] * _rms(mixed)


def _out(ya, zr, x2, gate, g_v, bs_mat, wts, layer, *, seq_len, tm):
    tokens, d = x2.shape
    assert gate.shape[0] == 1 or seq_len % tm == 0
    mod_row = (lambda i: 0) if gate.shape[0] == 1 else (lambda i: (i * tm) // seq_len)
    col0 = SSD_W // MLP_W
    const2 = lambda i: (0, 0)
    return pl.pallas_call(
        _out_kernel,
        grid=(tokens // tm,),
        in_specs=[
            pl.BlockSpec((tm, SSD_W), lambda i: (i, 0)),
            pl.BlockSpec((tm, MLP_W), lambda i: (i, col0)),
            pl.BlockSpec((tm, MLP_W), lambda i: (i, col0 + 1)),
            pl.BlockSpec((tm, MLP_W), lambda i: (i, col0 + 2)),
            pl.BlockSpec((tm, d), lambda i: (i, 0)),
            pl.BlockSpec((1, 1, d), lambda i: (mod_row(i), 0, 0)),
            pl.BlockSpec((1, MLP_W), const2),
            pl.BlockSpec((None, MLP_GROUPS, MLP_CHUNK, MLP_CHUNK), lambda i: (layer, 0, 0, 0)),
            pl.BlockSpec((MLP_CHUNK, MLP_W), const2),
            pl.BlockSpec((None, SSD_W + MLP_W, d), lambda i: (layer, 0, 0), pipeline_mode=pl.Buffered(1)),
        ],
        out_specs=pl.BlockSpec((tm, d), lambda i: (i, 0)),
        out_shape=jax.ShapeDtypeStruct((tokens, d), F32),
        scratch_shapes=[pltpu.VMEM((tm, MLP_W), F32)],
        compiler_params=pltpu.CompilerParams(
            dimension_semantics=("parallel",), vmem_limit_bytes=VMEM_LIMIT),
        name="mix_out",
    )(ya, zr, zr, zr, x2, gate, g_v, wts["w_s"], bs_mat, wts["w_out"])


def _expand_matrix():
    j = np.arange(LANES)[:, None]
    h = np.arange(SSD_W)[None, :] // HEAD_DIM
    return jnp.asarray((j % HEADS == h) & (j < 3 * HEADS), dtype=BF16)


def _stream_layer(x2, wts, p, layer, mod_rows, h0f, h0b, *, n_seq, seq_len, row_len, tm_in, tm_out):
    d = x2.shape[1]
    shift = mod_rows[:, None, 0:d]
    gs = p["g_pre"] * (1.0 + mod_rows[:, None, d:2 * d])
    gate = mod_rows[:, None, 2 * d:3 * d] * p["g_post"]
    zx, zr, dt = _inproj(x2, gs, shift, wts, layer, seq_len=seq_len, row_len=row_len, tm=tm_in)
    ck = min(SCAN_CHUNKS, seq_len // CHUNK)
    hb_starts, hb_fin, cum, rowp, vals = _bwd_scan(zx, dt, p["bias"], p["a_neg"], p["e_mat"], h0b,
                                                   n_seq=n_seq, seq_len=seq_len, ck=ck)
    ya, hf_fin = _ssd(zx, zr, cum, rowp, vals, p["e_mat"], h0f, hb_starts,
                      p["dskip"], n_seq=n_seq, seq_len=seq_len, ck=ck)
    x_new = _out(ya, zr, x2, gate, p["g_v"], p["bs_mat"], wts, layer,
                 seq_len=seq_len, tm=tm_out)
    return x_new, hf_fin, hb_fin


def _all_layer_weights(w_in, conv_w, conv_b, g_ssd, g_mlp, w_s, w_out):
    out_gain = jnp.concatenate([g_ssd, g_mlp], axis=1)[:, :, None]
    w16 = lax.optimization_barrier(jnp.swapaxes(w_in, 1, 2).astype(BF16))
    return {
        "w_conv": w16,
        "w_plain": w16.reshape(-1, w16.shape[2]),
        "w_dt": w16,
        "cw8": jnp.pad(conv_w, ((0, 0), (0, 8 - CONV_W), (0, 0))),
        "cb": conv_b[:, None, :],
        "w_s": w_s.astype(BF16),
        "w_out": (w_out * out_gain).astype(BF16),
    }


def _layer_params(l, g_pre, g_post, dt_bias, a_log, d_skip, g_v, b_s, e_mat):
    pad = jnp.zeros((LANES - DT_W,), F32)
    return {
        "g_pre": g_pre[l][None, None, :],
        "g_post": g_post[l][None, :],
        "bias": jnp.concatenate([dt_bias[l].reshape(-1), pad])[None, :],
        "a_neg": jnp.concatenate([-jnp.exp(a_log[l].reshape(-1)), pad])[None, :],
        "dskip": jnp.repeat(d_skip[l], HEAD_DIM)[None, :],
        "g_v": g_v[l][None, :],
        "bs_mat": jnp.repeat(b_s[l].T, MLP_GROUP_DIM, axis=1),
        "e_mat": e_mat,
    }


def kernel(x, c, ctx, c_ctx, w_ada, b_ada, g_pre, g_post, w_in, conv_w, conv_b, dt_bias, a_log,
           d_skip, g_ssd, g_v, w_s, b_s, g_mlp, w_out):
    bsz, seq, d = x.shape
    ctx_len = ctx.shape[1]
    depth = w_in.shape[0]
    assert seq % LATENT_TM_IN == 0 and ctx_len % CHUNK == 0 and ctx_len & (ctx_len - 1) == 0
    assert w_in.shape[2] == MAIN_W + DT_W and bsz + 1 <= 8

    cc = jnp.concatenate([c, c_ctx[None, :], jnp.zeros((8 - bsz - 1, d), F32)], axis=0)
    mod = _modulation(cc, w_ada, b_ada)
    e_mat = _expand_matrix()
    wts = _all_layer_weights(w_in, conv_w, conv_b, g_ssd, g_mlp, w_s, w_out)

    xs = x.reshape(bsz * seq, d)
    cs = ctx.reshape(bsz * ctx_len, d)
    zeros_h = jnp.zeros((bsz, GN, GROUP_W), F32)
    for l in range(depth):
        p = _layer_params(l, g_pre, g_post, dt_bias, a_log, d_skip, g_v, b_s, e_mat)
        cs, h_f, h_b = _stream_layer(cs, wts, p, l, mod[l, bsz:bsz + 1], zeros_h, zeros_h, n_seq=bsz,
                                     seq_len=ctx_len, row_len=ctx_len, tm_in=bsz * ctx_len, tm_out=ctx_len)
        xs, _, _ = _stream_layer(xs, wts, p, l, mod[l, :bsz], h_f, h_b, n_seq=bsz, seq_len=seq,
                                 row_len=GRID_W, tm_in=LATENT_TM_IN, tm_out=LATENT_TM_OUT)
    return xs.reshape(bsz, seq, d)
```

```python
import functools

import jax
import jax.numpy as jnp
import numpy as np
from jax import lax
from jax.experimental import pallas as pl
from jax.experimental.pallas import tpu as pltpu

F32 = jnp.float32
BF16 = jnp.bfloat16

GRID_W = 64
HEADS = 32
HEAD_DIM = 64
GROUPS = 8
HEADS_PER_GROUP = HEADS // GROUPS
STATE = 128
CHUNK = 128
CONV_W = 5
CONV_PAD = CONV_W // 2
MLP_GROUPS = 16
MLP_GROUP_DIM = 128
MLP_CHUNK = 128
EPS = 1e-6
LOG2E = 1.4426950408889634

SSD_W = HEADS * HEAD_DIM
GN = GROUPS * STATE
XBC_W = SSD_W + 2 * GN
DT_W = 2 * HEADS
MLP_W = MLP_GROUPS * MLP_GROUP_DIM
GROUP_W = HEADS_PER_GROUP * HEAD_DIM
REST_W = SSD_W + 3 * MLP_W
MAIN_W = XBC_W + REST_W
INPROJ_STEPS = 8
CONV_SLAB = XBC_W // INPROJ_STEPS
PLAIN_SLAB = REST_W // INPROJ_STEPS
LANES = 128
SCAN_CHUNKS = 4
LATENT_TM_IN = 1024
LATENT_TM_OUT = 512

VMEM_LIMIT = 56 * 1024 * 1024


def _silu(v):
    h = 0.5 * v
    return h * (1.0 + jnp.tanh(h))


def _rms(v):
    return v * lax.rsqrt(jnp.mean(v * v, axis=-1, keepdims=True) + EPS)


def _dot_nt(a, b):
    return lax.dot_general(a, b, (((1,), (1,)), ((), ())), preferred_element_type=F32)


def _split3(v):
    v1 = v.astype(BF16)
    r1 = v - v1.astype(F32)
    v2 = r1.astype(BF16)
    r2 = r1 - v2.astype(F32)
    return v1, v2, r2.astype(BF16)


def _mod_kernel(cc_ref, w_ref, b_ref, o_ref):
    sc = _silu(cc_ref[...])
    o_ref[0] = jnp.dot(sc, w_ref[0], preferred_element_type=F32) + b_ref[0]


def _modulation(cc, w_ada, b_ada):
    depth, d, d3 = w_ada.shape
    tn = 768
    return pl.pallas_call(
        _mod_kernel,
        grid=(depth, d3 // tn),
        in_specs=[
            pl.BlockSpec((8, d), lambda l, j: (0, 0)),
            pl.BlockSpec((1, d, tn), lambda l, j: (l, 0, j)),
            pl.BlockSpec((1, 1, tn), lambda l, j: (l, 0, j)),
        ],
        out_specs=pl.BlockSpec((1, 8, tn), lambda l, j: (l, 0, j)),
        out_shape=jax.ShapeDtypeStruct((depth, 8, d3), F32),
        compiler_params=pltpu.CompilerParams(
            dimension_semantics=("parallel", "parallel"), vmem_limit_bytes=VMEM_LIMIT),
        name="adaln_mod",
    )(cc, w_ada, b_ada.reshape(depth, 1, d3))


def _inproj_kernel(x_ref, gs_ref, sh_ref, wc_ref, wp_ref, wdt_ref, cw_ref, cb_ref,
                   zx_ref, zr_ref, dt_ref, hx_ref, *, row_len):
    j = pl.program_id(1)
    tm, tc = zx_ref.shape

    @pl.when(j == 0)
    def _():
        hb = (_rms(x_ref[...]) * gs_ref[0] + sh_ref[0]).astype(BF16)
        hx_ref[...] = hb
        dt_ref[...] = _dot_nt(hb, wdt_ref[...])

    hx = hx_ref[...]
    acc = _dot_nt(hx, wc_ref[...])
    zr_ref[...] = _dot_nt(hx, wp_ref[...]).astype(BF16)

    rows3 = acc.reshape(tm // row_len, row_len, tc)
    y = cb_ref[...] + cw_ref[CONV_PAD:CONV_PAD + 1, :] * acc
    for d in (-2, -1, 1, 2):
        pad = jnp.zeros((tm // row_len, abs(d), tc), F32)
        if d < 0:
            shifted = jnp.concatenate([pad, rows3[:, :row_len + d, :]], axis=1)
        else:
            shifted = jnp.concatenate([rows3[:, d:, :], pad], axis=1)
        y = y + cw_ref[CONV_PAD + d:CONV_PAD + d + 1, :] * shifted.reshape(tm, tc)
    zx_ref[...] = _silu(y).astype(BF16)


def _inproj(x2, gs, shift, wts, layer, *, seq_len, row_len, tm):
    tokens, d = x2.shape
    assert gs.shape[0] == 1 or seq_len % tm == 0
    mod_row = (lambda i: 0) if gs.shape[0] == 1 else (lambda i: (i * tm) // seq_len)
    kern = functools.partial(_inproj_kernel, row_len=row_len)
    return pl.pallas_call(
        kern,
        grid=(tokens // tm, INPROJ_STEPS),
        in_specs=[
            pl.BlockSpec((tm, d), lambda i, j: (i, 0)),
            pl.BlockSpec((1, 1, d), lambda i, j: (mod_row(i), 0, 0)),
            pl.BlockSpec((1, 1, d), lambda i, j: (mod_row(i), 0, 0)),
            pl.BlockSpec((None, CONV_SLAB, d), lambda i, j: (layer, j, 0)),
            pl.BlockSpec((pl.Element(PLAIN_SLAB), pl.Element(d)),
                         lambda i, j: (pl.multiple_of(
                             layer * (MAIN_W + DT_W) + XBC_W + DT_W + j * PLAIN_SLAB, DT_W), 0)),
            pl.BlockSpec((None, LANES, d), lambda i, j: (layer, XBC_W // LANES, 0)),
            pl.BlockSpec((None, 8, CONV_SLAB), lambda i, j: (layer, 0, j)),
            pl.BlockSpec((None, 1, CONV_SLAB), lambda i, j: (layer, 0, j)),
        ],
        out_specs=[
            pl.BlockSpec((tm, CONV_SLAB), lambda i, j: (i, j)),
            pl.BlockSpec((tm, PLAIN_SLAB), lambda i, j: (i, j)),
            pl.BlockSpec((tm, LANES), lambda i, j: (i, 0)),
        ],
        out_shape=[
            jax.ShapeDtypeStruct((tokens, XBC_W), BF16),
            jax.ShapeDtypeStruct((tokens, REST_W), BF16),
            jax.ShapeDtypeStruct((tokens, LANES), F32),
        ],
        scratch_shapes=[pltpu.VMEM((tm, d), BF16)],
        compiler_params=pltpu.CompilerParams(
            dimension_semantics=("parallel", "arbitrary"), vmem_limit_bytes=VMEM_LIMIT),
        name="inproj",
    )(x2, gs, shift, wts["w_conv"], wts["w_plain"], wts["w_dt"], wts["cw8"], wts["cb"])


def _chunk_prep(dtr, bias, a_neg):
    q = dtr.shape[0]
    xx = dtr + bias
    dt = jnp.maximum(xx, 0.0) + jnp.log1p(jnp.exp(-jnp.abs(xx)))
    a = dt * a_neg * LOG2E
    acat = jnp.concatenate(_split3(a), axis=0)
    row = lax.broadcasted_iota(jnp.int32, (q, q), 0)
    col = lax.broadcasted_iota(jnp.int32, (q, q), 1)
    t_inc = jnp.where(col <= row, 1.0, 0.0).astype(BF16)
    t_suf = jnp.where(col >= row, 1.0, 0.0).astype(BF16)
    incl = jnp.dot(jnp.concatenate([t_inc] * 3, axis=1), acat, preferred_element_type=F32)
    suf = jnp.dot(jnp.concatenate([t_suf] * 3, axis=1), acat, preferred_element_type=F32)
    lane = lax.broadcasted_iota(jnp.int32, (q, LANES), 1)
    fwd = lane < HEADS
    cum = jnp.where(fwd, incl, suf)
    tot = jnp.where(fwd, incl[q - 1:q, :], suf[0:1, :])
    w = dt * jnp.exp2(tot - cum)
    lo = lane < DT_W
    dt_sum = dt + pltpu.roll(dt, LANES - HEADS, axis=1)
    rowp = jnp.where(lo, cum - jnp.log2(dt), pltpu.roll(-jnp.log2(dt_sum), DT_W, axis=1))
    vals = jnp.where(lo, jnp.exp2(cum), pltpu.roll(w, DT_W, axis=1))
    return cum, rowp, vals


def _expand(vals, wanted, e_ref):
    v1 = vals.astype(BF16).astype(F32)
    r1 = vals - v1
    v2 = r1.astype(BF16).astype(F32)
    terms = (v1, v2, r1 - v2)
    packed = []
    for which, rows in wanted:
        n = rows.stop - rows.start
        lane = lax.broadcasted_iota(jnp.int32, (n, LANES), 1)
        p = jnp.zeros((n, LANES), F32)
        for k in (2, 1, 0):
            t = terms[k][rows]
            shift = (HEADS * (k - which)) % LANES
            t = t if shift == 0 else pltpu.roll(t, shift, axis=1)
            p = jnp.where(lane < HEADS * (k + 1), t, p)
        packed.append(p.astype(BF16))
    stacked = packed[0] if len(packed) == 1 else jnp.concatenate(packed, axis=0)
    return jnp.dot(stacked, e_ref[...], preferred_element_type=F32)


def _state_outer(b_g, xw_g):
    return lax.dot_general(b_g, xw_g, (((0,), (0,)), ((), ())), preferred_element_type=F32)


def _bwd_kernel(x_ref, b_ref, dt_ref, bias_ref, a_ref, e_ref, h0_ref,
                hs_ref, hfin_ref, cum_ref, rowp_ref, vals_ref, st_ref, *, ck):
    s = pl.program_id(1)

    @pl.when(s == 0)
    def _():
        st_ref[...] = h0_ref[0]

    q = CHUNK
    vals_all = []
    for ci in range(ck):
        tok = slice(ci * q, (ci + 1) * q)
        cum, rowp, vals = _chunk_prep(dt_ref[tok, :], bias_ref[...], a_ref[...])
        cum_ref[tok, :] = cum
        rowp_ref[tok, :] = rowp
        vals_ref[tok, :] = vals
        vals_all.append(vals)
    wanted = []
    for ci in range(ck):
        wanted += [(3, slice(ci * q, (ci + 1) * q)), (1, slice(ci * q, ci * q + 16))]
    ex_all = _expand(jnp.concatenate(vals_all, axis=0), wanted, e_ref)
    per_chunk = q + 16

    for ci in reversed(range(ck)):
        tok = slice(ci * q, (ci + 1) * q)
        base = ci * per_chunk
        hs_ref[0, ci] = st_ref[...].astype(BF16)
        for g in range(GROUPS):
            rows = slice(g * STATE, (g + 1) * STATE)
            cols = slice(g * GROUP_W, (g + 1) * GROUP_W)
            xw = (x_ref[tok, cols].astype(F32) * ex_all[base:base + q, cols]).astype(BF16)
            dec = ex_all[base + q:base + q + 1, cols]
            st_ref[rows, :] = st_ref[rows, :] * dec + _state_outer(b_ref[tok, rows], xw)

    @pl.when(s == pl.num_programs(1) - 1)
    def _():
        hfin_ref[0] = st_ref[...]


def _bwd_scan(zx, dt, bias, a_neg, e_mat, h0, *, n_seq, seq_len, ck):
    ns = seq_len // (ck * CHUNK)
    nc = seq_len // CHUNK
    tm = ck * CHUNK
    tok = lambda b, s: b * ns + (ns - 1 - s)
    return pl.pallas_call(
        functools.partial(_bwd_kernel, ck=ck),
        grid=(n_seq, ns),
        in_specs=[
            pl.BlockSpec((tm, SSD_W), lambda b, s: (tok(b, s), 0)),
            pl.BlockSpec((tm, GN), lambda b, s: (tok(b, s), SSD_W // GN)),
            pl.BlockSpec((tm, LANES), lambda b, s: (tok(b, s), 0)),
            pl.BlockSpec((1, LANES), lambda b, s: (0, 0)),
            pl.BlockSpec((1, LANES), lambda b, s: (0, 0)),
            pl.BlockSpec((LANES, SSD_W), lambda b, s: (0, 0)),
            pl.BlockSpec((1, GN, GROUP_W), lambda b, s: (b, 0, 0)),
        ],
        out_specs=[
            pl.BlockSpec((1, ck, GN, GROUP_W), lambda b, s: (b, ns - 1 - s, 0, 0)),
            pl.BlockSpec((1, GN, GROUP_W), lambda b, s: (b, 0, 0)),
            pl.BlockSpec((tm, LANES), lambda b, s: (tok(b, s), 0)),
            pl.BlockSpec((tm, LANES), lambda b, s: (tok(b, s), 0)),
            pl.BlockSpec((tm, LANES), lambda b, s: (tok(b, s), 0)),
        ],
        out_shape=[
            jax.ShapeDtypeStruct((n_seq, nc, GN, GROUP_W), BF16),
            jax.ShapeDtypeStruct((n_seq, GN, GROUP_W), F32),
            jax.ShapeDtypeStruct((n_seq * seq_len, LANES), F32),
            jax.ShapeDtypeStruct((n_seq * seq_len, LANES), F32),
            jax.ShapeDtypeStruct((n_seq * seq_len, LANES), F32),
        ],
        scratch_shapes=[pltpu.VMEM((GN, GROUP_W), F32)],
        compiler_params=pltpu.CompilerParams(
            dimension_semantics=("parallel", "arbitrary"), vmem_limit_bytes=VMEM_LIMIT),
        name="ssd_bwd_scan",
    )(zx, zx, dt, bias, a_neg, e_mat, h0)


def _ssd_kernel(x_ref, b_ref, c_ref, zs_ref, cum_ref, rowp_ref, vals_ref, e_ref, h0_ref, hb_ref,
                dskip_ref, ya_ref, hfin_ref, st_ref, y_ref, *, ck):
    s = pl.program_id(1)
    q = CHUNK

    @pl.when(s == 0)
    def _():
        st_ref[...] = h0_ref[0]

    row = lax.broadcasted_iota(jnp.int32, (q, q), 0)
    col = lax.broadcasted_iota(jnp.int32, (q, q), 1)
    lower = col < row
    upper = col > row
    first_head = lax.broadcasted_iota(jnp.int32, (q, 2 * HEAD_DIM), 1) < HEAD_DIM
    lane_pick = jnp.where(lower, 0, jnp.where(upper, HEADS, 2 * HEADS)).astype(jnp.int32)

    for ci in range(ck):
        tok = slice(ci * q, (ci + 1) * q)
        cum, vals = cum_ref[tok, :], vals_ref[tok, :]
        row_t = rowp_ref[tok, :].T
        ex = _expand(vals, [(0, slice(0, q)), (1, slice(0, q)), (2, slice(0, q))], e_ref)

        for g in range(GROUPS):
            rows = slice(g * STATE, (g + 1) * STATE)
            cols = slice(g * GROUP_W, (g + 1) * GROUP_W)
            x_g = x_ref[tok, cols]
            xf_g = x_g.astype(F32)
            b_g = b_ref[tok, rows]
            c_g = c_ref[tok, rows]
            cb = lax.dot_general(c_g, b_g, (((1,), (1,)), ((), ())), preferred_element_type=F32)
            ms = []
            for r in range(HEADS_PER_GROUP):
                h = g * HEADS_PER_GROUP + r
                col_term = jnp.take_along_axis(cum, lane_pick + h, axis=1)
                row_term = jnp.where(lower, row_t[h:h + 1, :],
                                     jnp.where(upper, row_t[HEADS + h:HEADS + h + 1, :],
                                               row_t[2 * HEADS + h:2 * HEADS + h + 1, :]))
                ms.append((cb * jnp.exp2(col_term - row_term)).astype(BF16))
            ys = []
            for pr in range(HEADS_PER_GROUP // 2):
                xp = x_g[:, pr * 2 * HEAD_DIM:(pr + 1) * 2 * HEAD_DIM]
                zero = jnp.zeros_like(xp)
                rhs = jnp.concatenate([jnp.where(first_head, xp, zero), jnp.where(first_head, zero, xp)], axis=0)
                lhs = jnp.concatenate([ms[2 * pr], ms[2 * pr + 1]], axis=1)
                ys.append(jnp.dot(lhs, rhs, preferred_element_type=F32))
            y_diag = jnp.concatenate(ys, axis=1)
            e_f = ex[0:q, cols]
            y_off = (jnp.dot(c_g, st_ref[rows, :].astype(BF16), preferred_element_type=F32) * e_f
                     + jnp.dot(c_g, hb_ref[0, ci, rows, :], preferred_element_type=F32) * ex[q:2 * q, cols])
            y_ref[tok, cols] = y_diag + y_off + dskip_ref[:, cols] * xf_g
            xw = (xf_g * ex[2 * q:3 * q, cols]).astype(BF16)
            st_ref[rows, :] = st_ref[rows, :] * e_f[q - 1:q, :] + _state_outer(b_g, xw)

        gated = y_ref[tok, :] * _silu(zs_ref[tok, :].astype(F32))
        ya_ref[tok, :] = _rms(gated).astype(BF16)

    @pl.when(s == pl.num_programs(1) - 1)
    def _():
        hfin_ref[0] = st_ref[...]


def _ssd(zx, zr, cum, rowp, vals, e_mat, h0f, hb_starts, dskip, *, n_seq, seq_len, ck):
    ns = seq_len // (ck * CHUNK)
    tm = ck * CHUNK
    tok = lambda b, s: b * ns + s
    tokens = n_seq * seq_len
    return pl.pallas_call(
        functools.partial(_ssd_kernel, ck=ck),
        grid=(n_seq, ns),
        in_specs=[
            pl.BlockSpec((tm, SSD_W), lambda b, s: (tok(b, s), 0)),
            pl.BlockSpec((tm, GN), lambda b, s: (tok(b, s), SSD_W // GN)),
            pl.BlockSpec((tm, GN), lambda b, s: (tok(b, s), SSD_W // GN + 1)),
            pl.BlockSpec((tm, SSD_W), lambda b, s: (tok(b, s), 0)),
            pl.BlockSpec((tm, LANES), lambda b, s: (tok(b, s), 0)),
            pl.BlockSpec((tm, LANES), lambda b, s: (tok(b, s), 0)),
            pl.BlockSpec((tm, LANES), lambda b, s: (tok(b, s), 0)),
            pl.BlockSpec((LANES, SSD_W), lambda b, s: (0, 0)),
            pl.BlockSpec((1, GN, GROUP_W), lambda b, s: (b, 0, 0)),
            pl.BlockSpec((1, ck, GN, GROUP_W), lambda b, s: (b, s, 0, 0)),
            pl.BlockSpec((1, SSD_W), lambda b, s: (0, 0)),
        ],
        out_specs=[
            pl.BlockSpec((tm, SSD_W), lambda b, s: (tok(b, s), 0)),
            pl.BlockSpec((1, GN, GROUP_W), lambda b, s: (b, 0, 0)),
        ],
        out_shape=[
            jax.ShapeDtypeStruct((tokens, SSD_W), BF16),
            jax.ShapeDtypeStruct((n_seq, GN, GROUP_W), F32),
        ],
        scratch_shapes=[pltpu.VMEM((GN, GROUP_W), F32), pltpu.VMEM((tm, SSD_W), F32)],
        compiler_params=pltpu.CompilerParams(
            dimension_semantics=("parallel", "arbitrary"), vmem_limit_bytes=VMEM_LIMIT),
        name="ssd_main",
    )(zx, zx, zx, zr, cum, rowp, vals, e_mat, h0f, hb_starts, dskip)


def _out_kernel(ya_ref, u_ref, v_ref, zm_ref, x_ref, gate_ref, gv_ref,
                ws_ref, bs_ref, wout_ref, o_ref, sg_ref, yb_ref):
    tm = x_ref.shape[0]
    for c in range(tm // MLP_CHUNK):
        rows = slice(c * MLP_CHUNK, (c + 1) * MLP_CHUNK)
        vn = (_rms(v_ref[rows, :].astype(F32)) * gv_ref[...]).astype(BF16)
        for g in range(MLP_GROUPS):
            cols = slice(g * MLP_GROUP_DIM, (g + 1) * MLP_GROUP_DIM)
            sg_ref[:, cols] = jnp.dot(ws_ref[g], vn[:, cols], preferred_element_type=F32) + bs_ref[:, cols]
        yb_ref[rows, :] = _rms(u_ref[rows, :].astype(F32) * sg_ref[...]
                               * _silu(zm_ref[rows, :].astype(F32))).astype(BF16)
    mixed = (jnp.dot(ya_ref[...], wout_ref[0:SSD_W, :], preferred_element_type=F32)
             + jnp.dot(yb_ref[...], wout_ref[SSD_W:, :], preferred_element_type=F32))
    o_ref[...] = x_ref[...] + gate_ref[0] * _rms(mixed)


def _out(ya, zr, x2, gate, g_v, bs_mat, wts, layer, *, seq_len, tm):
    tokens, d = x2.shape
    assert gate.shape[0] == 1 or seq_len % tm == 0
    mod_row = (lambda i: 0) if gate.shape[0] == 1 else (lambda i: (i * tm) // seq_len)
    col0 = SSD_W // MLP_W
    const2 = lambda i: (0, 0)
    return pl.pallas_call(
        _out_kernel,
        grid=(tokens // tm,),
        in_specs=[
            pl.BlockSpec((tm, SSD_W), lambda i: (i, 0)),
            pl.BlockSpec((tm, MLP_W), lambda i: (i, col0)),
            pl.BlockSpec((tm, MLP_W), lambda i: (i, col0 + 1)),
            pl.BlockSpec((tm, MLP_W), lambda i: (i, col0 + 2)),
            pl.BlockSpec((tm, d), lambda i: (i, 0)),
            pl.BlockSpec((1, 1, d), lambda i: (mod_row(i), 0, 0)),
            pl.BlockSpec((1, MLP_W), const2),
            pl.BlockSpec((None, MLP_GROUPS, MLP_CHUNK, MLP_CHUNK), lambda i: (layer, 0, 0, 0)),
            pl.BlockSpec((MLP_CHUNK, MLP_W), const2),
            pl.BlockSpec((None, SSD_W + MLP_W, d), lambda i: (layer, 0, 0), pipeline_mode=pl.Buffered(1)),
        ],
        out_specs=pl.BlockSpec((tm, d), lambda i: (i, 0)),
        out_shape=jax.ShapeDtypeStruct((tokens, d), F32),
        scratch_shapes=[pltpu.VMEM((MLP_CHUNK, MLP_W), F32), pltpu.VMEM((tm, MLP_W), BF16)],
        compiler_params=pltpu.CompilerParams(
            dimension_semantics=("parallel",), vmem_limit_bytes=VMEM_LIMIT),
        name="mix_out",
    )(ya, zr, zr, zr, x2, gate, g_v, wts["w_s"], bs_mat, wts["w_out"])


def _expand_matrix():
    j = np.arange(LANES)[:, None]
    h = np.arange(SSD_W)[None, :] // HEAD_DIM
    return jnp.asarray((j % HEADS == h) & (j < 3 * HEADS), dtype=BF16)


def _stream_layer(x2, wts, p, layer, mod_rows, h0f, h0b, *, n_seq, seq_len, row_len, tm_in, tm_out):
    d = x2.shape[1]
    shift = mod_rows[:, None, 0:d]
    gs = p["g_pre"] * (1.0 + mod_rows[:, None, d:2 * d])
    gate = mod_rows[:, None, 2 * d:3 * d] * p["g_post"]
    zx, zr, dt = _inproj(x2, gs, shift, wts, layer, seq_len=seq_len, row_len=row_len, tm=tm_in)
    ck = min(SCAN_CHUNKS, seq_len // CHUNK)
    hb_starts, hb_fin, cum, rowp, vals = _bwd_scan(zx, dt, p["bias"], p["a_neg"], p["e_mat"], h0b,
                                                   n_seq=n_seq, seq_len=seq_len, ck=ck)
    ya, hf_fin = _ssd(zx, zr, cum, rowp, vals, p["e_mat"], h0f, hb_starts,
                      p["dskip"], n_seq=n_seq, seq_len=seq_len, ck=ck)
    x_new = _out(ya, zr, x2, gate, p["g_v"], p["bs_mat"], wts, layer,
                 seq_len=seq_len, tm=tm_out)
    return x_new, hf_fin, hb_fin


def _all_layer_weights(w_in, conv_w, conv_b, g_ssd, g_mlp, w_s, w_out):
    out_gain = jnp.concatenate([g_ssd, g_mlp], axis=1)[:, :, None]
    w16 = lax.optimization_barrier(jnp.swapaxes(w_in, 1, 2).astype(BF16))
    return {
        "w_conv": w16,
        "w_plain": w16.reshape(-1, w16.shape[2]),
        "w_dt": w16,
        "cw8": jnp.pad(conv_w, ((0, 0), (0, 8 - CONV_W), (0, 0))),
        "cb": conv_b[:, None, :],
        "w_s": w_s.astype(BF16),
        "w_out": (w_out * out_gain).astype(BF16),
    }


def _layer_params(l, g_pre, g_post, dt_bias, a_log, d_skip, g_v, b_s, e_mat):
    pad = jnp.zeros((LANES - DT_W,), F32)
    return {
        "g_pre": g_pre[l][None, None, :],
        "g_post": g_post[l][None, :],
        "bias": jnp.concatenate([dt_bias[l].reshape(-1), pad])[None, :],
        "a_neg": jnp.concatenate([-jnp.exp(a_log[l].reshape(-1)), pad])[None, :],
        "dskip": jnp.repeat(d_skip[l], HEAD_DIM)[None, :],
        "g_v": g_v[l][None, :],
        "bs_mat": jnp.repeat(b_s[l].T, MLP_GROUP_DIM, axis=1),
        "e_mat": e_mat,
    }


def kernel(x, c, ctx, c_ctx, w_ada, b_ada, g_pre, g_post, w_in, conv_w, conv_b, dt_bias, a_log,
           d_skip, g_ssd, g_v, w_s, b_s, g_mlp, w_out):
    bsz, seq, d = x.shape
    ctx_len = ctx.shape[1]
    depth = w_in.shape[0]
    assert seq % LATENT_TM_IN == 0 and ctx_len % CHUNK == 0 and ctx_len & (ctx_len - 1) == 0
    assert w_in.shape[2] == MAIN_W + DT_W and bsz + 1 <= 8

    cc = jnp.concatenate([c, c_ctx[None, :], jnp.zeros((8 - bsz - 1, d), F32)], axis=0)
    mod = _modulation(cc, w_ada, b_ada)
    e_mat = _expand_matrix()
    wts = _all_layer_weights(w_in, conv_w, conv_b, g_ssd, g_mlp, w_s, w_out)

    xs = x.reshape(bsz * seq, d)
    cs = ctx.reshape(bsz * ctx_len, d)
    zeros_h = jnp.zeros((bsz, GN, GROUP_W), F32)
    for l in range(depth):
        p = _layer_params(l, g_pre, g_post, dt_bias, a_log, d_skip, g_v, b_s, e_mat)
        cs, h_f, h_b = _stream_layer(cs, wts, p, l, mod[l, bsz:bsz + 1], zeros_h, zeros_h, n_seq=bsz,
                                     seq_len=ctx_len, row_len=ctx_len, tm_in=bsz * ctx_len, tm_out=ctx_len)
        xs, _, _ = _stream_layer(xs, wts, p, l, mod[l, :bsz], h_f, h_b, n_seq=bsz, seq_len=seq,
                                 row_len=GRID_W, tm_in=LATENT_TM_IN, tm_out=LATENT_TM_OUT)
    return xs.reshape(bsz, seq, d)
```
